```python
import jax
import jax.numpy as jnp
from jax import lax
import numpy as np

D_MODEL = 1024
BATCH = 8
SEQ = 4096
DEPTH = 1
DEC_BATCH = 8
DEC_SEQ = 8192
PAST_LEN = 128

GRID_W = 64
N_MEM = 256
D_FF = 2816
NORM_EPS = 1e-6

NA_HEADS = 8
NA_HD = 64
NA_WIN_R = 8
NA_WIN_C = 16
D_NA = NA_HEADS * NA_HD

RW_HEADS = 8
RW_HD = 64
D_RW = RW_HEADS * RW_HD
N_DIR = 2
DECAY_LORA = 64
AAA_LORA = 64
GATE_LORA = 160
RW_CONV = 3
GN_EPS = 64e-5

MEM_HEADS = 4
MEM_HD = 128
D_MEM = MEM_HEADS * MEM_HD

N_BRANCH = 3

C_NA = 3 * D_NA
C_RW = 3 * D_RW + N_DIR * DECAY_LORA + N_DIR * AAA_LORA + GATE_LORA
C_MQ = D_MEM
C_GATE = N_BRANCH * D_MODEL
N_IN = C_NA + C_RW + C_MQ + C_GATE
IN_SPLITS = (C_NA, C_NA + C_RW, C_NA + C_RW + C_MQ)
RW_SPLITS = (D_RW, 2 * D_RW, 3 * D_RW, 3 * D_RW + N_DIR * DECAY_LORA,
             3 * D_RW + N_DIR * (DECAY_LORA + AAA_LORA))

kernel_name = 'hybrid_natten_rwkv7_mem_encoder'


def rmsnorm(x, g, eps=NORM_EPS):
    xf = x.astype(jnp.float32)
    y = xf * lax.rsqrt(jnp.mean(xf * xf, axis=-1, keepdims=True) + eps)
    return (y * g.astype(jnp.float32)).astype(x.dtype)


def swiglu(h, w_gate, w_up, w_down):
    return (jax.nn.silu(h @ w_gate) * (h @ w_up)) @ w_down


def neighbourhood_attention(q, k, v, rpb):
    B, S, H, hd = q.shape
    rows = S // GRID_W
    wr = min(NA_WIN_R, rows)
    wc = NA_WIN_C
    qg = q.reshape(B, rows, GRID_W, H, hd)
    kg = k.reshape(B, rows, GRID_W, H, hd)
    vg = v.reshape(B, rows, GRID_W, H, hd)
    cols = np.arange(GRID_W)
    col_start = np.clip(cols - wc // 2, 0, GRID_W - wc)
    col_idx = col_start[:, None] + np.arange(wc)[None, :]
    dc = col_idx - cols[:, None]
    bias_c = rpb[:, :, dc + NA_WIN_C - 1]
    scale = hd ** -0.5

    def row_block(r):
        r0 = jnp.clip(r - wr // 2, 0, rows - wr)
        k_rows = lax.dynamic_slice_in_dim(kg, r0, wr, axis=1)
        v_rows = lax.dynamic_slice_in_dim(vg, r0, wr, axis=1)
        k_win = k_rows[:, :, col_idx]
        v_win = v_rows[:, :, col_idx]
        q_r = lax.dynamic_index_in_dim(qg, r, axis=1, keepdims=False)
        s = jnp.einsum('bchd,brcjhd->bhcrj', q_r, k_win).astype(jnp.float32) * scale
        dr = r0 + jnp.arange(wr) - r + NA_WIN_R - 1
        bias = jnp.take(bias_c, dr, axis=1).transpose(0, 2, 1, 3)
        s = s + bias[None].astype(jnp.float32)
        p = jax.nn.softmax(s.reshape(B, H, GRID_W, wr * wc), axis=-1)
        p = p.reshape(B, H, GRID_W, wr, wc).astype(v.dtype)
        return jnp.einsum('bhcrj,brcjhd->bchd', p, v_win)

    out = lax.map(row_block, jnp.arange(rows))
    return out.transpose(1, 0, 2, 3, 4).reshape(B, S, H * hd)


def centred_short_conv(x, w):
    C = x.shape[-1]
    K = w.shape[0]
    return lax.conv_general_dilated(
        x, w.astype(x.dtype)[:, None, :], window_strides=(1,),
        padding=[(K // 2, K // 2)], dimension_numbers=('NWC', 'WIO', 'NWC'),
        feature_group_count=C)


def rwkv7_bidirectional(z, conv_rw, w0_rw, w2_rw, a0_rw, a2_rw, g2_rw, k_k_rw, k_a_rw,
                        r_k_rw, ln_x_w_rw, ln_x_b_rw):
    B, S, _ = z.shape
    H, N = RW_HEADS, RW_HD
    f32 = jnp.float32
    z = centred_short_conv(z, conv_rw).astype(f32)
    r, k, v, xw, xa, xg = jnp.split(z, RW_SPLITS, axis=-1)
    xw = xw.reshape(B, S, N_DIR, DECAY_LORA)
    xa = xa.reshape(B, S, N_DIR, AAA_LORA)
    w = -jax.nn.softplus(-(w0_rw + jnp.einsum('bsdl,dlc->bsdc', jnp.tanh(xw), w2_rw))) - 0.5
    decay = jnp.exp(-jnp.exp(w))
    a = jax.nn.sigmoid(a0_rw + jnp.einsum('bsdl,dlc->bsdc', xa, a2_rw))
    g = jax.nn.sigmoid(xg) @ g2_rw
    kk = (k * k_k_rw).reshape(B, S, H, N)
    kk = kk / jnp.maximum(jnp.sqrt(jnp.sum(kk * kk, axis=-1, keepdims=True)), 1e-12)
    kd = k[:, :, None] * (1.0 + (a - 1.0) * k_a_rw)

    def heads(t):
        return t.reshape(t.shape[:-1] + (H, N))

    r, v = heads(r), heads(v)
    kd, decay, a = heads(kd), heads(decay), heads(a)

    def per_dir(t):
        t = jnp.stack([t[:, :, 0], jnp.flip(t[:, :, 1], axis=1)], axis=2)
        return t.transpose(1, 2, 0, 3, 4)

    def both(t):
        return per_dir(jnp.broadcast_to(t[:, :, None], (B, S, N_DIR, H, N)))

    def step(state, inp):
        r_t, w_t, k_t, v_t, kk_t, a_t = inp
        sa = jnp.einsum('dbhvk,dbhk->dbhv', state, -kk_t)
        state = (state * w_t[..., None, :]
                 + sa[..., :, None] * (kk_t * a_t)[..., None, :]
                 + v_t[..., :, None] * k_t[..., None, :])
        y = jnp.einsum('dbhvk,dbhk->dbhv', state, r_t)
        return state, y

    xs = (both(r), per_dir(decay), per_dir(kd), both(v), both(kk), per_dir(a))
    state0 = jnp.zeros((N_DIR, B, H, N, N), f32)
    _, ys = lax.scan(step, state0, xs)
    y = (ys[:, 0] + jnp.flip(ys[:, 1], axis=0)).transpose(1, 0, 2, 3)
    mu = jnp.mean(y, axis=-1, keepdims=True)
    var = jnp.mean(jnp.square(y - mu), axis=-1, keepdims=True)
    y = ((y - mu) * lax.rsqrt(var + GN_EPS)).reshape(B, S, D_RW) * ln_x_w_rw + ln_x_b_rw
    bonus = jnp.einsum('bshn,bsdhn,dhn->bsh', r, kd, r_k_rw)[..., None] * v
    y = y + bonus.reshape(B, S, D_RW)
    return y * g


def memory_attention(zq, mem, g_mem_norm, w_mem_kv, g_qn_mem, g_kn_mem):
    B, S, _ = zq.shape
    kv = rmsnorm(mem, g_mem_norm) @ w_mem_kv
    km, vm = jnp.split(kv, 2, axis=-1)
    M = mem.shape[1]
    q = rmsnorm(zq.reshape(B, S, MEM_HEADS, MEM_HD), g_qn_mem)
    km = rmsnorm(km.reshape(B, M, MEM_HEADS, MEM_HD), g_kn_mem)
    vm = vm.reshape(B, M, MEM_HEADS, MEM_HD)
    s = jnp.einsum('bshd,bmhd->bhsm', q, km).astype(jnp.float32) * (MEM_HD ** -0.5)
    p = jax.nn.softmax(s, axis=-1).astype(vm.dtype)
    return jnp.einsum('bhsm,bmhd->bshd', p, vm).reshape(B, S, D_MEM)


def encoder_layer(x, mem, p):
    B, S, _ = x.shape
    x = x + 0.5 * swiglu(rmsnorm(x, p['g_ffn1']), p['w_ffn1_gate'], p['w_ffn1_up'], p['w_ffn1_down'])
    h = rmsnorm(x, p['g_mix'])
    z = h @ p['w_in']
    z_na, z_rw, z_mq, z_gate = jnp.split(z, IN_SPLITS, axis=-1)
    qa, ka, va = jnp.split(z_na, 3, axis=-1)
    qa = rmsnorm(qa.reshape(B, S, NA_HEADS, NA_HD), p['g_qn_na'])
    ka = rmsnorm(ka.reshape(B, S, NA_HEADS, NA_HD), p['g_kn_na'])
    va = va.reshape(B, S, NA_HEADS, NA_HD)
    y_na = neighbourhood_attention(qa, ka, va, p['rpb_na']) @ p['w_o_na']
    y_rw = rwkv7_bidirectional(z_rw, p['conv_rw'], p['w0_rw'], p['w2_rw'], p['a0_rw'], p['a2_rw'],
                               p['g2_rw'], p['k_k_rw'], p['k_a_rw'], p['r_k_rw'],
                               p['ln_x_w_rw'], p['ln_x_b_rw']).astype(x.dtype) @ p['w_o_rw']
    y_mem = memory_attention(z_mq, mem, p['g_mem_norm'], p['w_mem_kv'],
                             p['g_qn_mem'], p['g_kn_mem']) @ p['w_o_mem']
    gates = jax.nn.sigmoid(z_gate).reshape(B, S, N_BRANCH, D_MODEL)
    merged = gates[:, :, 0] * y_na + gates[:, :, 1] * y_rw + gates[:, :, 2] * y_mem
    x = x + merged @ p['w_out']
    x = x + 0.5 * swiglu(rmsnorm(x, p['g_ffn2']), p['w_ffn2_gate'], p['w_ffn2_up'], p['w_ffn2_down'])
    return x


def setup_inputs(seed: int = 0):
    key = jax.random.key(seed)
    ks = iter(jax.random.split(key, 48))
    f32 = jnp.float32
    L = DEPTH

    def nrm(shape, scale):
        return scale * jax.random.normal(next(ks), shape, f32)

    def gain(shape):
        return 1.0 + 0.02 * jax.random.normal(next(ks), shape, f32)

    conv_base = jnp.array([0.25, 1.0, 0.25], f32)[None, :, None]
    return {
        'x_prompt': nrm((BATCH, SEQ, D_MODEL), 1.0),
        'x_sample': nrm((DEC_BATCH, DEC_SEQ, D_MODEL), 1.0),
        'mem_prompt': nrm((BATCH, N_MEM, D_MODEL), 1.0),
        'mem_sample': nrm((DEC_BATCH, N_MEM, D_MODEL), 1.0),
        'g_ffn1': gain((L, D_MODEL)),
        'w_ffn1_gate': nrm((L, D_MODEL, D_FF), D_MODEL ** -0.5),
        'w_ffn1_up': nrm((L, D_MODEL, D_FF), D_MODEL ** -0.5),
        'w_ffn1_down': nrm((L, D_FF, D_MODEL), D_FF ** -0.5),
        'g_mix': gain((L, D_MODEL)),
        'w_in': nrm((L, D_MODEL, N_IN), D_MODEL ** -0.5),
        'g_qn_na': gain((L, NA_HD)),
        'g_kn_na': gain((L, NA_HD)),
        'rpb_na': nrm((L, NA_HEADS, 2 * NA_WIN_R - 1, 2 * NA_WIN_C - 1), 0.02),
        'w_o_na': nrm((L, D_NA, D_MODEL), D_NA ** -0.5),
        'conv_rw': conv_base + nrm((L, RW_CONV, C_RW), 0.05),
        'w0_rw': jax.random.uniform(next(ks), (L, N_DIR, D_RW), f32, -6.0, 0.0),
        'w2_rw': nrm((L, N_DIR, DECAY_LORA, D_RW), 0.1 * DECAY_LORA ** -0.5),
        'a0_rw': nrm((L, N_DIR, D_RW), 0.1),
        'a2_rw': nrm((L, N_DIR, AAA_LORA, D_RW), 0.1 * AAA_LORA ** -0.5),
        'g2_rw': nrm((L, GATE_LORA, D_RW), GATE_LORA ** -0.5),
        'k_k_rw': 0.85 + nrm((L, D_RW), 0.02),
        'k_a_rw': gain((L, D_RW)),
        'r_k_rw': nrm((L, N_DIR, RW_HEADS, RW_HD), 0.1),
        'ln_x_w_rw': gain((L, D_RW)),
        'ln_x_b_rw': nrm((L, D_RW), 0.02),
        'w_o_rw': nrm((L, D_RW, D_MODEL), D_RW ** -0.5),
        'g_mem_norm': gain((L, D_MODEL)),
        'w_mem_kv': nrm((L, D_MODEL, 2 * D_MEM), D_MODEL ** -0.5),
        'g_qn_mem': gain((L, MEM_HD)),
        'g_kn_mem': gain((L, MEM_HD)),
        'w_o_mem': nrm((L, D_MEM, D_MODEL), D_MEM ** -0.5),
        'w_out': nrm((L, D_MODEL, D_MODEL), D_MODEL ** -0.5),
        'g_ffn2': gain((L, D_MODEL)),
        'w_ffn2_gate': nrm((L, D_MODEL, D_FF), D_MODEL ** -0.5),
        'w_ffn2_up': nrm((L, D_MODEL, D_FF), D_MODEL ** -0.5),
        'w_ffn2_down': nrm((L, D_FF, D_MODEL), D_FF ** -0.5),
    }


def reference(x_prompt, x_sample, mem_prompt, mem_sample,
              g_ffn1, w_ffn1_gate, w_ffn1_up, w_ffn1_down,
              g_mix, w_in,
              g_qn_na, g_kn_na, rpb_na, w_o_na,
              conv_rw, w0_rw, w2_rw, a0_rw, a2_rw, g2_rw, k_k_rw, k_a_rw, r_k_rw,
              ln_x_w_rw, ln_x_b_rw, w_o_rw,
              g_mem_norm, w_mem_kv, g_qn_mem, g_kn_mem, w_o_mem,
              w_out,
              g_ffn2, w_ffn2_gate, w_ffn2_up, w_ffn2_down):
    layer_params = {
        'g_ffn1': g_ffn1, 'w_ffn1_gate': w_ffn1_gate, 'w_ffn1_up': w_ffn1_up,
        'w_ffn1_down': w_ffn1_down, 'g_mix': g_mix, 'w_in': w_in,
        'g_qn_na': g_qn_na, 'g_kn_na': g_kn_na, 'rpb_na': rpb_na, 'w_o_na': w_o_na,
        'conv_rw': conv_rw, 'w0_rw': w0_rw, 'w2_rw': w2_rw, 'a0_rw': a0_rw, 'a2_rw': a2_rw,
        'g2_rw': g2_rw, 'k_k_rw': k_k_rw, 'k_a_rw': k_a_rw, 'r_k_rw': r_k_rw,
        'ln_x_w_rw': ln_x_w_rw, 'ln_x_b_rw': ln_x_b_rw, 'w_o_rw': w_o_rw,
        'g_mem_norm': g_mem_norm, 'w_mem_kv': w_mem_kv, 'g_qn_mem': g_qn_mem,
        'g_kn_mem': g_kn_mem, 'w_o_mem': w_o_mem, 'w_out': w_out,
        'g_ffn2': g_ffn2, 'w_ffn2_gate': w_ffn2_gate, 'w_ffn2_up': w_ffn2_up,
        'w_ffn2_down': w_ffn2_down,
    }
    y_prompt = x_prompt
    y_sample = x_sample
    for l in range(DEPTH):
        p = {name: arr[l] for name, arr in layer_params.items()}
        y_prompt = encoder_layer(y_prompt, mem_prompt, p)
        y_sample = encoder_layer(y_sample, mem_sample, p)
    return (y_prompt, y_sample)
```

```python
import functools

import jax
import jax.numpy as jnp
import numpy as np
from jax import lax
from jax.experimental import pallas as pl
from jax.experimental.pallas import tpu as pltpu

F32 = jnp.float32
BF16 = jnp.bfloat16

D_MODEL = 1024
D_FF = 2816
NORM_EPS = 1e-6
GRID_W = 64
N_MEM = 256
NA_HEADS, NA_HD, NA_WIN_R, NA_WIN_C = 8, 64, 8, 16
D_NA = NA_HEADS * NA_HD
RW_HEADS, RW_HD = 8, 64
D_RW = RW_HEADS * RW_HD
DECAY_LORA, AAA_LORA, GATE_LORA = 64, 64, 160
GN_EPS = 64e-5
MEM_HEADS, MEM_HD = 4, 128
D_MEM = MEM_HEADS * MEM_HD
C_NA = 3 * D_NA
C_RW = 3 * D_RW + 2 * DECAY_LORA + 2 * AAA_LORA + GATE_LORA
C_RW_PAD = 2048
C_GATE = 3 * D_MODEL
N_IN_PAD = C_NA + C_RW_PAD + D_MEM + C_GATE

TOKEN_TILE = 512
FF_CHUNK = 256
NA_ROWS_PER_STEP = 8
RW_CHUNK = 64
LANES = 128
VMEM_LIMIT = 56 * 1024 * 1024
NEG_BIG = -1e30


def _const_spec(shape):
    nd = len(shape)
    return pl.BlockSpec(shape, lambda *_: (0,) * nd, pipeline_mode=pl.Buffered(1))


def _dot(a, b):
    return jnp.dot(a, b, preferred_element_type=F32)


def _dot_nt(a, b):
    return lax.dot_general(a, b, (((1,), (1,)), ((), ())), preferred_element_type=F32)


def _dot_tn(a, b):
    return lax.dot_general(a, b, (((0,), (0,)), ((), ())), preferred_element_type=F32)


def _split_dot(x, m, parts):
    acc = None
    rem = x
    for i in range(parts):
        piece = rem.astype(BF16)
        term = _dot(piece, m)
        acc = term if acc is None else acc + term
        if i + 1 < parts:
            rem = rem - piece.astype(F32)
    return acc


def _rmsnorm_rows(x, g):
    ms = jnp.mean(x * x, axis=-1, keepdims=True)
    return x * lax.rsqrt(ms + NORM_EPS) * g


def _ffn_kernel(x_ref, g_ref, wg_ref, wu_ref, wd_ref, o_ref, a_scr):
    x = x_ref[...]
    h = _rmsnorm_rows(x, g_ref[...]).astype(BF16)
    for f in range(0, D_FF, FF_CHUNK):
        gate = _dot(h, wg_ref[:, f:f + FF_CHUNK])
        up = _dot(h, wu_ref[:, f:f + FF_CHUNK])
        a_scr[:, f:f + FF_CHUNK] = (gate * jax.nn.sigmoid(gate) * up).astype(BF16)
    o_ref[...] = x + 0.5 * _dot(a_scr[...], wd_ref[...])


def _ffn(x, g, wg, wu, wd):
    t = x.shape[0]
    tm = TOKEN_TILE
    return pl.pallas_call(
        _ffn_kernel,
        out_shape=jax.ShapeDtypeStruct((t, D_MODEL), F32),
        grid=(t // tm,),
        in_specs=[
            pl.BlockSpec((tm, D_MODEL), lambda i: (i, 0)),
            _const_spec((1, D_MODEL)),
            _const_spec((D_MODEL, D_FF)),
            _const_spec((D_MODEL, D_FF)),
            _const_spec((D_FF, D_MODEL)),
        ],
        out_specs=pl.BlockSpec((tm, D_MODEL), lambda i: (i, 0)),
        scratch_shapes=[pltpu.VMEM((tm, D_FF), BF16)],
        compiler_params=pltpu.CompilerParams(
            dimension_semantics=("parallel",), vmem_limit_bytes=VMEM_LIMIT),
        name="ffn",
    )(x, g, wg, wu, wd)


def _proj_kernel(x_ref, g_ref, w_ref, gq_ref, gk_ref, gmq_ref, bd_ref,
                 q_ref, k_ref, v_ref, zrw_ref, mq_ref, gate_ref):
    h = _rmsnorm_rows(x_ref[...], g_ref[...]).astype(BF16)
    bd = bd_ref[...]

    def head_norm(z, gain):
        ms = _split_dot(z * z, bd, 2) * (1.0 / NA_HD)
        return z * lax.rsqrt(ms + NORM_EPS) * gain

    zq = _dot(h, w_ref[:, 0:D_NA])
    q_ref[...] = (head_norm(zq, gq_ref[...]) * (NA_HD ** -0.5)).astype(BF16)
    zk = _dot(h, w_ref[:, D_NA:2 * D_NA])
    k_ref[...] = head_norm(zk, gk_ref[...]).astype(BF16)
    v_ref[...] = _dot(h, w_ref[:, 2 * D_NA:3 * D_NA]).astype(BF16)

    for c in range(0, C_RW_PAD, 512):
        zrw_ref[:, c:c + 512] = _dot(h, w_ref[:, C_NA + c:C_NA + c + 512])

    off = C_NA + C_RW_PAD
    zm = _dot(h, w_ref[:, off:off + D_MEM])
    gmq = gmq_ref[...]
    for hd in range(MEM_HEADS):
        sl = slice(hd * MEM_HD, (hd + 1) * MEM_HD)
        mq_ref[:, sl] = (_rmsnorm_rows(zm[:, sl], gmq) * (MEM_HD ** -0.5)).astype(BF16)

    off += D_MEM
    for c in range(0, C_GATE, 512):
        zg = _dot(h, w_ref[:, off + c:off + c + 512])
        gate_ref[:, c:c + 512] = jax.nn.sigmoid(zg).astype(BF16)


def _proj(x, g, w_in_p, gq, gk, gmq, bd64):
    t = x.shape[0]
    tm = TOKEN_TILE
    tok = lambda n: pl.BlockSpec((tm, n), lambda i: (i, 0))
    return pl.pallas_call(
        _proj_kernel,
        out_shape=(
            jax.ShapeDtypeStruct((t, D_NA), BF16),
            jax.ShapeDtypeStruct((t, D_NA), BF16),
            jax.ShapeDtypeStruct((t, D_NA), BF16),
            jax.ShapeDtypeStruct((t, C_RW_PAD), F32),
            jax.ShapeDtypeStruct((t, D_MEM), BF16),
            jax.ShapeDtypeStruct((t, C_GATE), BF16),
        ),
        grid=(t // tm,),
        in_specs=[
            tok(D_MODEL),
            _const_spec((1, D_MODEL)),
            _const_spec((D_MODEL, N_IN_PAD)),
            _const_spec((1, D_NA)),
            _const_spec((1, D_NA)),
            _const_spec((1, MEM_HD)),
            _const_spec((D_NA, D_NA)),
        ],
        out_specs=(tok(D_NA), tok(D_NA), tok(D_NA), tok(C_RW_PAD), tok(D_MEM), tok(C_GATE)),
        compiler_params=pltpu.CompilerParams(
            dimension_semantics=("parallel",), vmem_limit_bytes=VMEM_LIMIT),
        name="proj",
    )(x, g, w_in_p, gq, gk, gmq, bd64)


def _na_kernel(q_ref, k_ref, v_ref, bias_ref, o_ref, *, rows):
    j = pl.program_id(1)
    lane = lax.broadcasted_iota(jnp.int32, (GRID_W, LANES), 1)
    head0 = lane < NA_HD
    nkeys = NA_WIN_R * GRID_W

    def row_body(rr, carry):
        r = j * NA_ROWS_PER_STEP + rr
        r0 = jnp.clip(r - NA_WIN_R // 2, 0, rows - NA_WIN_R)
        d = r - r0
        qoff = pl.multiple_of(rr * GRID_W, GRID_W)
        koff = pl.multiple_of(r0 * GRID_W, GRID_W)
        for p in range(NA_HEADS // 2):
            ls = slice(p * LANES, (p + 1) * LANES)
            qp = q_ref[pl.ds(qoff, GRID_W), ls]
            zero = jnp.zeros_like(qp)
            qst = jnp.concatenate([jnp.where(head0, qp, zero), jnp.where(head0, zero, qp)], axis=0)
            kp = k_ref[pl.ds(koff, nkeys), ls]
            vp = v_ref[pl.ds(koff, nkeys), ls]
            s = _dot_nt(qst, kp) + bias_ref[d, p]
            m = jnp.max(s, axis=-1, keepdims=True)
            e = jnp.exp(s - m)
            prob = (e / jnp.sum(e, axis=-1, keepdims=True)).astype(BF16)
            o = _dot(prob, vp)
            o_ref[pl.ds(qoff, GRID_W), ls] = jnp.where(head0, o[:GRID_W], o[GRID_W:]).astype(BF16)
        return carry

    lax.fori_loop(0, NA_ROWS_PER_STEP, row_body, 0)


def _na(q, k, v, bias, batch, seq):
    rows = seq // GRID_W
    assert rows >= NA_WIN_R and rows % NA_ROWS_PER_STEP == 0
    steps = rows // NA_ROWS_PER_STEP
    tq = NA_ROWS_PER_STEP * GRID_W
    return pl.pallas_call(
        functools.partial(_na_kernel, rows=rows),
        out_shape=jax.ShapeDtypeStruct((batch * seq, D_NA), BF16),
        grid=(batch, steps),
        in_specs=[
            pl.BlockSpec((tq, D_NA), lambda b, j: (b * steps + j, 0)),
            pl.BlockSpec((seq, D_NA), lambda b, j: (b, 0)),
            pl.BlockSpec((seq, D_NA), lambda b, j: (b, 0)),
            _const_spec(bias.shape),
        ],
        out_specs=pl.BlockSpec((tq, D_NA), lambda b, j: (b * steps + j, 0)),
        compiler_params=pltpu.CompilerParams(
            dimension_semantics=("parallel", "arbitrary"), vmem_limit_bytes=VMEM_LIMIT),
        name="natten",
    )(q, k, v, bias)


def _na_bias_table(rpb):
    wc = NA_WIN_C
    cols = np.arange(GRID_W)
    col_start = np.clip(cols - wc // 2, 0, GRID_W - wc)
    cc = np.arange(GRID_W)
    valid = (cc[None, :] >= col_start[:, None]) & (cc[None, :] < col_start[:, None] + wc)
    dc = np.clip(cc[None, :] - cols[:, None] + NA_WIN_C - 1, 0, 2 * NA_WIN_C - 2)
    d = np.arange(NA_WIN_R)
    rr = np.arange(NA_WIN_R)
    dr = rr[None, :] - d[:, None] + NA_WIN_R - 1
    tab = rpb[:, dr][:, :, :, dc]
    tab = jnp.where(valid[None, None, None], tab, NEG_BIG)
    tab = tab.transpose(1, 0, 3, 2, 4).reshape(NA_WIN_R, NA_HEADS // 2, 2 * GRID_W, NA_WIN_R * GRID_W)
    return tab.astype(F32)


def _memkv_kernel(m_ref, g_ref, w_ref, gk_ref, km_ref, vm_ref):
    h = _rmsnorm_rows(m_ref[...], g_ref[...]).astype(BF16)
    kv = _dot(h, w_ref[...])
    gk = gk_ref[...]
    for hd in range(MEM_HEADS):
        sl = slice(hd * MEM_HD, (hd + 1) * MEM_HD)
        km_ref[:, sl] = _rmsnorm_rows(kv[:, sl], gk).astype(BF16)
    vm_ref[...] = kv[:, D_MEM:].astype(BF16)


def _memkv(mem2d, g, w, gk):
    n = mem2d.shape[0]
    blk = pl.BlockSpec((N_MEM, D_MEM), lambda i: (i, 0))
    return pl.pallas_call(
        _memkv_kernel,
        out_shape=(jax.ShapeDtypeStruct((n, D_MEM), BF16), jax.ShapeDtypeStruct((n, D_MEM), BF16)),
        grid=(n // N_MEM,),
        in_specs=[
            pl.BlockSpec((N_MEM, D_MODEL), lambda i: (i, 0)),
            _const_spec((1, D_MODEL)),
            _const_spec((D_MODEL, 2 * D_MEM)),
            _const_spec((1, MEM_HD)),
        ],
        out_specs=(blk, blk),
        compiler_params=pltpu.CompilerParams(
            dimension_semantics=("parallel",), vmem_limit_bytes=VMEM_LIMIT),
        name="memkv",
    )(mem2d, g, w, gk)


def _memattn_kernel(q_ref, km_ref, vm_ref, o_ref):
    for hd in range(MEM_HEADS):
        sl = slice(hd * MEM_HD, (hd + 1) * MEM_HD)
        s = _dot_nt(q_ref[:, sl], km_ref[:, sl])
        m = jnp.max(s, axis=-1, keepdims=True)
        e = jnp.exp(s - m)
        prob = (e / jnp.sum(e, axis=-1, keepdims=True)).astype(BF16)
        o_ref[:, sl] = _dot(prob, vm_ref[:, sl]).astype(BF16)


def _memattn(q, km, vm, seq):
    t = q.shape[0]
    tm = TOKEN_TILE
    assert seq % tm == 0
    mem_blk = pl.BlockSpec((N_MEM, D_MEM), lambda i: ((i * tm) // seq, 0))
    return pl.pallas_call(
        _memattn_kernel,
        out_shape=jax.ShapeDtypeStruct((t, D_MEM), BF16),
        grid=(t // tm,),
        in_specs=[pl.BlockSpec((tm, D_MEM), lambda i: (i, 0)), mem_blk, mem_blk],
        out_specs=pl.BlockSpec((tm, D_MEM), lambda i: (i, 0)),
        compiler_params=pltpu.CompilerParams(
            dimension_semantics=("parallel",), vmem_limit_bytes=VMEM_LIMIT),
        name="memattn",
    )(q, km, vm)


def _softplus(x):
    return jnp.maximum(x, 0.0) + jnp.log(1.0 + jnp.exp(-jnp.abs(x)))


def _rwkv_kernel(z_ref, zp_ref, zn_ref, conv_ref, w0_ref, w2_ref, a0_ref, a2_ref, g2_ref,
                 kk_ref, ka_ref, rk_ref, lnw_ref, lnb_ref, bd_ref, *rest, rev, nch):
    if rev:
        yf_ref, o_ref, state = rest
    else:
        o_ref, state = rest
    cn = RW_CHUNK
    j = pl.program_id(1)
    c = (nch - 1 - j) if rev else j
    d = 1 if rev else 0

    @pl.when(j == 0)
    def _():
        state[...] = jnp.zeros_like(state)

    row = lax.broadcasted_iota(jnp.int32, (cn, 1), 0)
    has_prev = (c > 0).astype(F32)
    has_next = (c < nch - 1).astype(F32)

    def conv(lo, hi):
        z = z_ref[:, lo:hi]
        zprev = zp_ref[7:8, lo:hi] * has_prev
        znext = zn_ref[0:1, lo:hi] * has_next
        zm1 = jnp.where(row == 0, zprev, pltpu.roll(z, 1, 0))
        zp1 = jnp.where(row == cn - 1, znext, pltpu.roll(z, cn - 1, 0))
        return zm1 * conv_ref[0:1, lo:hi] + z * conv_ref[1:2, lo:hi] + zp1 * conv_ref[2:3, lo:hi]

    bd = bd_ref[...]
    r = conv(0, D_RW)
    k = conv(D_RW, 2 * D_RW)
    v = conv(2 * D_RW, 3 * D_RW)
    xw = conv(3 * D_RW, 3 * D_RW + LANES)
    xa = conv(3 * D_RW + LANES, 3 * D_RW + 2 * LANES).astype(BF16)

    def in_context_rate(dd):
        return jax.nn.sigmoid(a0_ref[dd:dd + 1, :] + _dot(xa, a2_ref[dd]))

    wpre = w0_ref[d:d + 1, :] + _dot(jnp.tanh(xw).astype(BF16), w2_ref[d])
    logw = -jnp.exp(-_softplus(-wpre) - 0.5)
    a = in_context_rate(d)
    kkr = k * kk_ref[...]
    kk = kkr / jnp.maximum(jnp.sqrt(_split_dot(kkr * kkr, bd, 2)), 1e-12)
    kd = k * (1.0 + (a - 1.0) * ka_ref[...])
    alpha = -kk
    beta = kk * a

    ti = lax.broadcasted_iota(jnp.int32, (cn, cn), 0)
    ii = lax.broadcasted_iota(jnp.int32, (cn, cn), 1)
    incl = (ii >= ti) if rev else (ii <= ti)
    logp = _split_dot_left(jnp.where(incl, 1.0, 0.0).astype(BF16), logw)
    logpc = logp[0:1, :] if rev else logp[cn - 1:cn, :]
    at = alpha * jnp.exp(logp - logw)
    rt = r * jnp.exp(logp)
    einv = jnp.exp(-logp)
    bt = beta * einv
    kt = kd * einv
    efin = jnp.exp(logpc - logp)
    bh = beta * efin
    kh = kd * efin
    pc = jnp.exp(logpc)

    lane = lax.broadcasted_iota(jnp.int32, (cn, LANES), 1)
    head0 = lane < RW_HD

    def stack(x):
        zero = jnp.zeros_like(x)
        return jnp.concatenate([jnp.where(head0, x, zero), jnp.where(head0, zero, x)], axis=0)

    ri = lax.broadcasted_iota(jnp.int32, (LANES, LANES), 0)
    ci = lax.broadcasted_iota(jnp.int32, (LANES, LANES), 1)
    same_head = (ri < RW_HD) == (ci < RW_HD)
    strict_bd = same_head & ((ci > ri) if rev else (ci < ri))
    t2 = lax.broadcasted_iota(jnp.int32, (cn, LANES), 0)
    i2 = lax.broadcasted_iota(jnp.int32, (cn, LANES), 1) & (cn - 1)
    incl2 = (i2 >= t2) if rev else (i2 <= t2)
    eye = jnp.where(ri == ci, 1.0, 0.0)

    ys = []
    for p in range(RW_HEADS // 2):
        ls = slice(p * LANES, (p + 1) * LANES)
        sp = state[p]
        sp_b = sp.astype(BF16)
        at_p = at[:, ls].astype(BF16)
        rt_p = rt[:, ls].astype(BF16)
        v_p = v[:, ls].astype(BF16)
        ast = stack(at_p)
        bst = stack(bt[:, ls].astype(BF16))
        kst = stack(kt[:, ls].astype(BF16))
        vst = stack(v_p)
        a_ab = jnp.where(strict_bd, _dot_nt(ast, bst), 0.0)
        a_ak = jnp.where(strict_bd, _dot_nt(ast, kst), 0.0).astype(BF16)
        a_rb = jnp.where(incl2, _dot_nt(rt_p, bst), 0.0).astype(BF16)
        a_rk = jnp.where(incl2, _dot_nt(rt_p, kst), 0.0).astype(BF16)
        tinv = eye + a_ab
        pw = a_ab
        for _ in range(5):
            pw_b = pw.astype(BF16)
            pw = _dot(pw_b, pw_b)
            tinv = tinv + _dot(tinv.astype(BF16), pw.astype(BF16))
        xst = stack(_dot_nt(at_p, sp_b)) + _dot(a_ak, vst)
        ust = _dot(tinv.astype(BF16), xst.astype(BF16))
        ust_b = ust.astype(BF16)
        y = _dot_nt(rt_p, sp_b) + _dot(a_rb, ust_b) + _dot(a_rk, vst)
        u_p = (ust[:cn] + ust[cn:]).astype(BF16)
        upd = _dot_tn(u_p, bh[:, ls].astype(BF16)) + _dot_tn(v_p, kh[:, ls].astype(BF16))
        state[p] = sp * pc[:, ls] + jnp.where(same_head, upd, 0.0)
        ys.append(y)
    y = jnp.concatenate(ys, axis=1)

    if not rev:
        o_ref[...] = y
        return

    ytot = y + yf_ref[...]
    mu = _split_dot(ytot, bd, 2) * (1.0 / RW_HD)
    yc = ytot - mu
    var = _split_dot(yc * yc, bd, 2) * (1.0 / RW_HD)
    yn = yc * lax.rsqrt(var + GN_EPS) * lnw_ref[...] + lnb_ref[...]
    a_fwd = in_context_rate(0)
    kd_fwd = k * (1.0 + (a_fwd - 1.0) * ka_ref[...])
    bonus = _split_dot(r * (kd_fwd * rk_ref[0:1, :] + kd * rk_ref[1:2, :]), bd, 2) * v
    xg = conv(3 * D_RW + 2 * LANES, C_RW_PAD)
    g = _dot(jax.nn.sigmoid(xg).astype(BF16), g2_ref[...])
    o_ref[...] = ((yn + bonus) * g).astype(BF16)


def _split_dot_left(m, x):
    hi = x.astype(BF16)
    r1 = x - hi.astype(F32)
    mid = r1.astype(BF16)
    lo = (r1 - mid.astype(F32)).astype(BF16)
    return _dot(m, hi) + _dot(m, mid) + _dot(m, lo)


def _rwkv_pass(zrw, yf, wts, batch, seq, rev):
    cn = RW_CHUNK
    nch = seq // cn
    sub = cn // 8
    nblk8 = batch * seq // 8

    def chunk(j):
        return (nch - 1 - j) if rev else j

    main = lambda n: pl.BlockSpec((cn, n), lambda b, j: (b * nch + chunk(j), 0))
    prev = pl.BlockSpec((8, C_RW_PAD), lambda b, j: (jnp.maximum((b * nch + chunk(j)) * sub - 1, 0), 0))
    nxt = pl.BlockSpec((8, C_RW_PAD),
                       lambda b, j: (jnp.minimum((b * nch + chunk(j) + 1) * sub, nblk8 - 1), 0))
    in_specs = [main(C_RW_PAD), prev, nxt] + [_const_spec(w.shape) for w in wts]
    args = [zrw, zrw, zrw] + list(wts)
    if rev:
        in_specs.append(main(D_RW))
        args.append(yf)
    return pl.pallas_call(
        functools.partial(_rwkv_kernel, rev=rev, nch=nch),
        out_shape=jax.ShapeDtypeStruct((batch * seq, D_RW), BF16 if rev else F32),
        grid=(batch, nch),
        in_specs=in_specs,
        out_specs=main(D_RW),
        scratch_shapes=[pltpu.VMEM((RW_HEADS // 2, LANES, LANES), F32)],
        compiler_params=pltpu.CompilerParams(
            dimension_semantics=("parallel", "arbitrary"), vmem_limit_bytes=VMEM_LIMIT),
        name="rwkv_bwd" if rev else "rwkv_fwd",
    )(*args)


def _merge_kernel(x_ref, na_ref, rw_ref, mem_ref, gate_ref, wna_ref, wrw_ref, wmem_ref, wout_ref, o_ref):
    d = D_MODEL
    merged = (gate_ref[:, 0:d].astype(F32) * _dot(na_ref[...], wna_ref[...])
              + gate_ref[:, d:2 * d].astype(F32) * _dot(rw_ref[...], wrw_ref[...])
              + gate_ref[:, 2 * d:3 * d].astype(F32) * _dot(mem_ref[...], wmem_ref[...]))
    o_ref[...] = x_ref[...] + _dot(merged.astype(BF16), wout_ref[...])


def _merge(x, y_na, y_rw, y_mem, gates, wna, wrw, wmem, wout):
    t = x.shape[0]
    tm = TOKEN_TILE
    tok = lambda n: pl.BlockSpec((tm, n), lambda i: (i, 0))
    return pl.pallas_call(
        _merge_kernel,
        out_shape=jax.ShapeDtypeStruct((t, D_MODEL), F32),
        grid=(t // tm,),
        in_specs=[tok(D_MODEL), tok(D_NA), tok(D_RW), tok(D_MEM), tok(C_GATE),
                  _const_spec((D_NA, D_MODEL)), _const_spec((D_RW, D_MODEL)),
                  _const_spec((D_MEM, D_MODEL)), _const_spec((D_MODEL, D_MODEL))],
        out_specs=tok(D_MODEL),
        compiler_params=pltpu.CompilerParams(
            dimension_semantics=("parallel",), vmem_limit_bytes=VMEM_LIMIT),
        name="merge",
    )(x, y_na, y_rw, y_mem, gates, wna, wrw, wmem, wout)


def _block_diag_ones(n, blk):
    idx = np.arange(n) // blk
    return jnp.asarray(idx[:, None] == idx[None, :], dtype=BF16)


def _prep_params(p):
    row = lambda a: a.reshape(1, -1).astype(F32)
    w_in = p['w_in']
    rw_lo, rw_hi = C_NA, C_NA + C_RW
    w_in_p = jnp.concatenate([
        w_in[:, :C_NA],
        jnp.pad(w_in[:, rw_lo:rw_hi], ((0, 0), (0, C_RW_PAD - C_RW))),
        w_in[:, rw_hi:],
    ], axis=1).astype(BF16)

    def lora_stack(w):
        z = jnp.zeros_like(w[0])
        return jnp.stack([jnp.concatenate([w[0], z], axis=0),
                          jnp.concatenate([z, w[1]], axis=0)]).astype(BF16)

    q = {
        'g_ffn1': row(p['g_ffn1']), 'g_ffn2': row(p['g_ffn2']), 'g_mix': row(p['g_mix']),
        'wg1': p['w_ffn1_gate'].astype(BF16), 'wu1': p['w_ffn1_up'].astype(BF16),
        'wd1': p['w_ffn1_down'].astype(BF16),
        'wg2': p['w_ffn2_gate'].astype(BF16), 'wu2': p['w_ffn2_up'].astype(BF16),
        'wd2': p['w_ffn2_down'].astype(BF16),
        'w_in_p': w_in_p,
        'gq': row(jnp.tile(p['g_qn_na'], NA_HEADS)), 'gk': row(jnp.tile(p['g_kn_na'], NA_HEADS)),
        'gmq': row(p['g_qn_mem']), 'gmk': row(p['g_kn_mem']),
        'bd64': _block_diag_ones(D_NA, NA_HD),
        'na_bias': _na_bias_table(p['rpb_na'].astype(F32)),
        'g_mem_norm': row(p['g_mem_norm']), 'w_mem_kv': p['w_mem_kv'].astype(BF16),
        'w_o_na': p['w_o_na'].astype(BF16), 'w_o_rw': p['w_o_rw'].astype(BF16),
        'w_o_mem': p['w_o_mem'].astype(BF16), 'w_out': p['w_out'].astype(BF16),
    }
    q['rw'] = (
        jnp.pad(p['conv_rw'].astype(F32), ((0, 0), (0, C_RW_PAD - C_RW))),
        p['w0_rw'].astype(F32), lora_stack(p['w2_rw']),
        p['a0_rw'].astype(F32), lora_stack(p['a2_rw']),
        jnp.pad(p['g2_rw'], ((0, 2 * LANES - GATE_LORA), (0, 0))).astype(BF16),
        row(p['k_k_rw']), row(p['k_a_rw']), p['r_k_rw'].reshape(2, D_RW).astype(F32),
        row(p['ln_x_w_rw']), row(p['ln_x_b_rw']),
        q['bd64'],
    )
    return q


def _encoder_layer(x, mem, q):
    batch, seq, _ = x.shape
    x2d = x.reshape(batch * seq, D_MODEL)
    x1 = _ffn(x2d, q['g_ffn1'], q['wg1'], q['wu1'], q['wd1'])
    qn, kn, vv, zrw, mq, gates = _proj(x1, q['g_mix'], q['w_in_p'], q['gq'], q['gk'], q['gmq'], q['bd64'])
    y_na = _na(qn, kn, vv, q['na_bias'], batch, seq)
    km, vm = _memkv(mem.reshape(batch * N_MEM, D_MODEL), q['g_mem_norm'], q['w_mem_kv'], q['gmk'])
    y_mem = _memattn(mq, km, vm, seq)
    y_fwd = _rwkv_pass(zrw, None, q['rw'], batch, seq, rev=False)
    y_rw = _rwkv_pass(zrw, y_fwd, q['rw'], batch, seq, rev=True)
    x2 = _merge(x1, y_na, y_rw, y_mem, gates, q['w_o_na'], q['w_o_rw'], q['w_o_mem'], q['w_out'])
    x3 = _ffn(x2, q['g_ffn2'], q['wg2'], q['wu2'], q['wd2'])
    return x3.reshape(batch, seq, D_MODEL)


def kernel(x_prompt, x_sample, mem_prompt, mem_sample, g_ffn1, w_ffn1_gate, w_ffn1_up, w_ffn1_down, g_mix, w_in, g_qn_na, g_kn_na, rpb_na, w_o_na, conv_rw, w0_rw, w2_rw, a0_rw, a2_rw, g2_rw, k_k_rw, k_a_rw, r_k_rw, ln_x_w_rw, ln_x_b_rw, w_o_rw, g_mem_norm, w_mem_kv, g_qn_mem, g_kn_mem, w_o_mem, w_out, g_ffn2, w_ffn2_gate, w_ffn2_up, w_ffn2_down):
    params = {
        'g_ffn1': g_ffn1, 'w_ffn1_gate': w_ffn1_gate, 'w_ffn1_up': w_ffn1_up,
        'w_ffn1_down': w_ffn1_down, 'g_mix': g_mix, 'w_in': w_in,
        'g_qn_na': g_qn_na, 'g_kn_na': g_kn_na, 'rpb_na': rpb_na, 'w_o_na': w_o_na,
        'conv_rw': conv_rw, 'w0_rw': w0_rw, 'w2_rw': w2_rw, 'a0_rw': a0_rw, 'a2_rw': a2_rw,
        'g2_rw': g2_rw, 'k_k_rw': k_k_rw, 'k_a_rw': k_a_rw, 'r_k_rw': r_k_rw,
        'ln_x_w_rw': ln_x_w_rw, 'ln_x_b_rw': ln_x_b_rw, 'w_o_rw': w_o_rw,
        'g_mem_norm': g_mem_norm, 'w_mem_kv': w_mem_kv, 'g_qn_mem': g_qn_mem,
        'g_kn_mem': g_kn_mem, 'w_o_mem': w_o_mem, 'w_out': w_out,
        'g_ffn2': g_ffn2, 'w_ffn2_gate': w_ffn2_gate, 'w_ffn2_up': w_ffn2_up,
        'w_ffn2_down': w_ffn2_down,
    }
    depth = g_ffn1.shape[0]
    y_prompt, y_sample = x_prompt, x_sample
    for layer in range(depth):
        q = _prep_params({name: arr[layer] for name, arr in params.items()})
        y_prompt = _encoder_layer(y_prompt, mem_prompt, q)
        y_sample = _encoder_layer(y_sample, mem_sample, q)
    return (y_prompt, y_sample)
```

```python
import functools

import jax
import jax.numpy as jnp
import numpy as np
from jax import lax
from jax.experimental import pallas as pl
from jax.experimental.pallas import tpu as pltpu

F32 = jnp.float32
BF16 = jnp.bfloat16

D_MODEL = 1024
D_FF = 2816
NORM_EPS = 1e-6
GRID_W = 64
N_MEM = 256
NA_HEADS, NA_HD, NA_WIN_R, NA_WIN_C = 8, 64, 8, 16
D_NA = NA_HEADS * NA_HD
RW_HEADS, RW_HD = 8, 64
D_RW = RW_HEADS * RW_HD
DECAY_LORA, AAA_LORA, GATE_LORA = 64, 64, 160
GN_EPS = 64e-5
MEM_HEADS, MEM_HD = 4, 128
D_MEM = MEM_HEADS * MEM_HD
C_NA = 3 * D_NA
C_RW = 3 * D_RW + 2 * DECAY_LORA + 2 * AAA_LORA + GATE_LORA
C_RW_PAD = 2048
C_GATE = 3 * D_MODEL
N_IN_PAD = C_NA + C_RW_PAD + D_MEM + C_GATE

TOKEN_TILE = 512
FF_CHUNK = 256
NA_ROWS_PER_STEP = 8
NA_ROWS_UNROLL = 2
RW_CHUNK = 64
RW_STEP = 256
RW_GROUP = 2
LANES = 128
VMEM_LIMIT = 56 * 1024 * 1024
NEG_BIG = -1e30


def _const_spec(shape):
    nd = len(shape)
    return pl.BlockSpec(shape, lambda *_: (0,) * nd, pipeline_mode=pl.Buffered(1))


def _dot(a, b):
    return jnp.dot(a, b, preferred_element_type=F32)


def _dot_nt(a, b):
    return lax.dot_general(a, b, (((1,), (1,)), ((), ())), preferred_element_type=F32)


def _dot_tn(a, b):
    return lax.dot_general(a, b, (((0,), (0,)), ((), ())), preferred_element_type=F32)


def _split_dot(x, m, parts):
    acc = None
    rem = x
    for i in range(parts):
        piece = rem.astype(BF16)
        term = _dot(piece, m)
        acc = term if acc is None else acc + term
        if i + 1 < parts:
            rem = rem - piece.astype(F32)
    return acc


def _rmsnorm_rows(x, g):
    ms = jnp.mean(x * x, axis=-1, keepdims=True)
    return x * lax.rsqrt(ms + NORM_EPS) * g


def _ffn_kernel(x_ref, g_ref, wg_ref, wu_ref, wd_ref, o_ref, a_scr):
    x = x_ref[...]
    h = _rmsnorm_rows(x, g_ref[...]).astype(BF16)
    for f in range(0, D_FF, FF_CHUNK):
        gate = _dot(h, wg_ref[:, f:f + FF_CHUNK])
        up = _dot(h, wu_ref[:, f:f + FF_CHUNK])
        a_scr[:, f:f + FF_CHUNK] = (gate * jax.nn.sigmoid(gate) * up).astype(BF16)
    o_ref[...] = x + 0.5 * _dot(a_scr[...], wd_ref[...])


def _ffn(x, g, wg, wu, wd):
    t = x.shape[0]
    tm = TOKEN_TILE
    return pl.pallas_call(
        _ffn_kernel,
        out_shape=jax.ShapeDtypeStruct((t, D_MODEL), F32),
        grid=(t // tm,),
        in_specs=[
            pl.BlockSpec((tm, D_MODEL), lambda i: (i, 0)),
            _const_spec((1, D_MODEL)),
            _const_spec((D_MODEL, D_FF)),
            _const_spec((D_MODEL, D_FF)),
            _const_spec((D_FF, D_MODEL)),
        ],
        out_specs=pl.BlockSpec((tm, D_MODEL), lambda i: (i, 0)),
        scratch_shapes=[pltpu.VMEM((tm, D_FF), BF16)],
        compiler_params=pltpu.CompilerParams(
            dimension_semantics=("parallel",), vmem_limit_bytes=VMEM_LIMIT),
        name="ffn",
    )(x, g, wg, wu, wd)


def _proj_kernel(x_ref, g_ref, w_ref, gq_ref, gk_ref, gmq_ref, bd_ref,
                 q_ref, k_ref, v_ref, zrw_ref, mq_ref, gate_ref):
    h = _rmsnorm_rows(x_ref[...], g_ref[...]).astype(BF16)
    bd = bd_ref[...]

    def head_norm(z, gain):
        ms = _split_dot(z * z, bd, 2) * (1.0 / NA_HD)
        return z * lax.rsqrt(ms + NORM_EPS) * gain

    zq = _dot(h, w_ref[:, 0:D_NA])
    q_ref[...] = (head_norm(zq, gq_ref[...]) * (NA_HD ** -0.5)).astype(BF16)
    zk = _dot(h, w_ref[:, D_NA:2 * D_NA])
    k_ref[...] = head_norm(zk, gk_ref[...]).astype(BF16)
    v_ref[...] = _dot(h, w_ref[:, 2 * D_NA:3 * D_NA]).astype(BF16)

    for c in range(0, C_RW_PAD, 512):
        zrw_ref[:, c:c + 512] = _dot(h, w_ref[:, C_NA + c:C_NA + c + 512])

    off = C_NA + C_RW_PAD
    zm = _dot(h, w_ref[:, off:off + D_MEM])
    gmq = gmq_ref[...]
    for hd in range(MEM_HEADS):
        sl = slice(hd * MEM_HD, (hd + 1) * MEM_HD)
        mq_ref[:, sl] = (_rmsnorm_rows(zm[:, sl], gmq) * (MEM_HD ** -0.5)).astype(BF16)

    off += D_MEM
    for c in range(0, C_GATE, 512):
        zg = _dot(h, w_ref[:, off + c:off + c + 512])
        gate_ref[:, c:c + 512] = jax.nn.sigmoid(zg).astype(BF16)


def _proj(x, g, w_in_p, gq, gk, gmq, bd64):
    t = x.shape[0]
    tm = TOKEN_TILE
    tok = lambda n: pl.BlockSpec((tm, n), lambda i: (i, 0))
    return pl.pallas_call(
        _proj_kernel,
        out_shape=(
            jax.ShapeDtypeStruct((t, D_NA), BF16),
            jax.ShapeDtypeStruct((t, D_NA), BF16),
            jax.ShapeDtypeStruct((t, D_NA), BF16),
            jax.ShapeDtypeStruct((t, C_RW_PAD), F32),
            jax.ShapeDtypeStruct((t, D_MEM), BF16),
            jax.ShapeDtypeStruct((t, C_GATE), BF16),
        ),
        grid=(t // tm,),
        in_specs=[
            tok(D_MODEL),
            _const_spec((1, D_MODEL)),
            _const_spec((D_MODEL, N_IN_PAD)),
            _const_spec((1, D_NA)),
            _const_spec((1, D_NA)),
            _const_spec((1, MEM_HD)),
            _const_spec((D_NA, D_NA)),
        ],
        out_specs=(tok(D_NA), tok(D_NA), tok(D_NA), tok(C_RW_PAD), tok(D_MEM), tok(C_GATE)),
        compiler_params=pltpu.CompilerParams(
            dimension_semantics=("parallel",), vmem_limit_bytes=VMEM_LIMIT),
        name="proj",
    )(x, g, w_in_p, gq, gk, gmq, bd64)


def _na_kernel(q_ref, k_ref, v_ref, bias_ref, o_ref, *, rows):
    j = pl.program_id(1)
    lane = lax.broadcasted_iota(jnp.int32, (GRID_W, LANES), 1)
    head0 = lane < NA_HD
    nkeys = NA_WIN_R * GRID_W

    def rows_body(it, carry):
        probs = []
        for u in range(NA_ROWS_UNROLL):
            rr = it * NA_ROWS_UNROLL + u
            r = j * NA_ROWS_PER_STEP + rr
            r0 = jnp.clip(r - NA_WIN_R // 2, 0, rows - NA_WIN_R)
            qoff = pl.multiple_of(rr * GRID_W, GRID_W)
            koff = pl.multiple_of(r0 * GRID_W, GRID_W)
            for p in range(NA_HEADS // 2):
                probs.append((r - r0, qoff, koff, p, slice(p * LANES, (p + 1) * LANES)))
        scores = []
        for d, qoff, koff, p, ls in probs:
            qp = q_ref[pl.ds(qoff, GRID_W), ls]
            zero = jnp.zeros_like(qp)
            qst = jnp.concatenate([jnp.where(head0, qp, zero), jnp.where(head0, zero, qp)], axis=0)
            scores.append(_dot_nt(qst, k_ref[pl.ds(koff, nkeys), ls]) + bias_ref[d, p])
        weights, inv_sums = [], []
        for s in scores:
            e = jnp.exp(s - jnp.max(s, axis=-1, keepdims=True))
            inv_sums.append(1.0 / jnp.sum(e, axis=-1, keepdims=True))
            weights.append(e.astype(BF16))
        outs = [_dot(w, v_ref[pl.ds(koff, nkeys), ls]) * inv
                for w, inv, (d, qoff, koff, p, ls) in zip(weights, inv_sums, probs)]
        for o, (d, qoff, koff, p, ls) in zip(outs, probs):
            o_ref[pl.ds(qoff, GRID_W), ls] = jnp.where(head0, o[:GRID_W], o[GRID_W:]).astype(BF16)
        return carry

    lax.fori_loop(0, NA_ROWS_PER_STEP // NA_ROWS_UNROLL, rows_body, 0)


def _na(q, k, v, bias, batch, seq):
    rows = seq // GRID_W
    assert rows >= NA_WIN_R and rows % NA_ROWS_PER_STEP == 0
    steps = rows // NA_ROWS_PER_STEP
    tq = NA_ROWS_PER_STEP * GRID_W
    return pl.pallas_call(
        functools.partial(_na_kernel, rows=rows),
        out_shape=jax.ShapeDtypeStruct((batch * seq, D_NA), BF16),
        grid=(batch, steps),
        in_specs=[
            pl.BlockSpec((tq, D_NA), lambda b, j: (b * steps + j, 0)),
            pl.BlockSpec((seq, D_NA), lambda b, j: (b, 0)),
            pl.BlockSpec((seq, D_NA), lambda b, j: (b, 0)),
            _const_spec(bias.shape),
        ],
        out_specs=pl.BlockSpec((tq, D_NA), lambda b, j: (b * steps + j, 0)),
        compiler_params=pltpu.CompilerParams(
            dimension_semantics=("parallel", "arbitrary"), vmem_limit_bytes=VMEM_LIMIT),
        name="natten",
    )(q, k, v, bias)


def _na_bias_table(rpb):
    wc = NA_WIN_C
    cols = np.arange(GRID_W)
    col_start = np.clip(cols - wc // 2, 0, GRID_W - wc)
    cc = np.arange(GRID_W)
    valid = (cc[None, :] >= col_start[:, None]) & (cc[None, :] < col_start[:, None] + wc)
    dc = np.clip(cc[None, :] - cols[:, None] + NA_WIN_C - 1, 0, 2 * NA_WIN_C - 2)
    d = np.arange(NA_WIN_R)
    rr = np.arange(NA_WIN_R)
    dr = rr[None, :] - d[:, None] + NA_WIN_R - 1
    tab = rpb[:, dr][:, :, :, dc]
    tab = jnp.where(valid[None, None, None], tab, NEG_BIG)
    tab = tab.transpose(1, 0, 3, 2, 4).reshape(NA_WIN_R, NA_HEADS // 2, 2 * GRID_W, NA_WIN_R * GRID_W)
    return tab.astype(F32)


def _memkv_kernel(m_ref, g_ref, w_ref, gk_ref, km_ref, vm_ref):
    h = _rmsnorm_rows(m_ref[...], g_ref[...]).astype(BF16)
    kv = _dot(h, w_ref[...])
    gk = gk_ref[...]
    for hd in range(MEM_HEADS):
        sl = slice(hd * MEM_HD, (hd + 1) * MEM_HD)
        km_ref[:, sl] = _rmsnorm_rows(kv[:, sl], gk).astype(BF16)
    vm_ref[...] = kv[:, D_MEM:].astype(BF16)


def _memkv(mem2d, g, w, gk):
    n = mem2d.shape[0]
    blk = pl.BlockSpec((N_MEM, D_MEM), lambda i: (i, 0))
    return pl.pallas_call(
        _memkv_kernel,
        out_shape=(jax.ShapeDtypeStruct((n, D_MEM), BF16), jax.ShapeDtypeStruct((n, D_MEM), BF16)),
        grid=(n // N_MEM,),
        in_specs=[
            pl.BlockSpec((N_MEM, D_MODEL), lambda i: (i, 0)),
            _const_spec((1, D_MODEL)),
            _const_spec((D_MODEL, 2 * D_MEM)),
            _const_spec((1, MEM_HD)),
        ],
        out_specs=(blk, blk),
        compiler_params=pltpu.CompilerParams(
            dimension_semantics=("parallel",), vmem_limit_bytes=VMEM_LIMIT),
        name="memkv",
    )(mem2d, g, w, gk)


def _memattn_kernel(q_ref, km_ref, vm_ref, o_ref):
    for hd in range(MEM_HEADS):
        sl = slice(hd * MEM_HD, (hd + 1) * MEM_HD)
        s = _dot_nt(q_ref[:, sl], km_ref[:, sl])
        m = jnp.max(s, axis=-1, keepdims=True)
        e = jnp.exp(s - m)
        inv = 1.0 / jnp.sum(e, axis=-1, keepdims=True)
        o_ref[:, sl] = (_dot(e.astype(BF16), vm_ref[:, sl]) * inv).astype(BF16)


def _memattn(q, km, vm, seq):
    t = q.shape[0]
    tm = TOKEN_TILE
    assert seq % tm == 0
    mem_blk = pl.BlockSpec((N_MEM, D_MEM), lambda i: ((i * tm) // seq, 0))
    return pl.pallas_call(
        _memattn_kernel,
        out_shape=jax.ShapeDtypeStruct((t, D_MEM), BF16),
        grid=(t // tm,),
        in_specs=[pl.BlockSpec((tm, D_MEM), lambda i: (i, 0)), mem_blk, mem_blk],
        out_specs=pl.BlockSpec((tm, D_MEM), lambda i: (i, 0)),
        compiler_params=pltpu.CompilerParams(
            dimension_semantics=("parallel",), vmem_limit_bytes=VMEM_LIMIT),
        name="memattn",
    )(q, km, vm)


def _softplus(x):
    return jnp.maximum(x, 0.0) + jnp.log(1.0 + jnp.exp(-jnp.abs(x)))


def _rwkv_kernel(z_ref, zp_ref, zn_ref, conv_ref, w0_ref, w2_ref, a0_ref, a2_ref, g2_ref,
                 kk_ref, ka_ref, rk_ref, lnw_ref, lnb_ref, bd_ref, *rest, rev, nsteps):
    if rev:
        yf_ref, o_ref, *scr = rest
        bonus_s, gate_s = scr[-2:]
    else:
        o_ref, *scr = rest
    state, at_s, rt_s, bt_s, kt_s, bh_s, kh_s, v_s, rhat_s, yhat_s, m_s, g_s, pc_s = scr[:13]
    cn = RW_CHUNK
    ts = RW_STEP
    nck = ts // cn
    j = pl.program_id(1)
    c = (nsteps - 1 - j) if rev else j
    d = 1 if rev else 0

    @pl.when(j == 0)
    def _():
        state[...] = jnp.zeros_like(state)

    gr = RW_GROUP * cn
    ngroups = ts // gr
    row8 = lax.broadcasted_iota(jnp.int32, (8, 1), 0)
    has_prev = (c > 0).astype(F32)
    has_next = (c < nsteps - 1).astype(F32)
    bd = bd_ref[...]
    shift = cn.bit_length() - 1
    ti = lax.broadcasted_iota(jnp.int32, (gr, gr), 0)
    ii = lax.broadcasted_iota(jnp.int32, (gr, gr), 1)
    incl = ((ti >> shift) == (ii >> shift)) & ((ii >= ti) if rev else (ii <= ti))
    incl_b = jnp.where(incl, 1.0, 0.0).astype(BF16)

    def prepare(g):
        r0 = g * gr
        rows = slice(r0, r0 + gr)

        def conv(lo, hi):
            z = z_ref[rows, lo:hi]
            before = zp_ref[7:8, lo:hi] * has_prev if g == 0 else z_ref[r0 - 1:r0, lo:hi]
            after = (zn_ref[0:1, lo:hi] * has_next if g == ngroups - 1
                     else z_ref[r0 + gr:r0 + gr + 1, lo:hi])
            down = pltpu.roll(z, 1, 0)
            up = pltpu.roll(z, gr - 1, 0)
            zm1 = jnp.concatenate([jnp.where(row8 == 0, before, down[:8]), down[8:]], axis=0)
            zp1 = jnp.concatenate([up[:gr - 8], jnp.where(row8 == 7, after, up[gr - 8:])], axis=0)
            return zm1 * conv_ref[0:1, lo:hi] + z * conv_ref[1:2, lo:hi] + zp1 * conv_ref[2:3, lo:hi]

        r = conv(0, D_RW)
        yield
        k = conv(D_RW, 2 * D_RW)
        yield
        v = conv(2 * D_RW, 3 * D_RW)
        v_s[rows, :] = v.astype(BF16)
        yield
        xw = conv(3 * D_RW, 3 * D_RW + LANES)
        xa = conv(3 * D_RW + LANES, 3 * D_RW + 2 * LANES).astype(BF16)

        def in_context_rate(dd):
            return jax.nn.sigmoid(a0_ref[dd:dd + 1, :] + _dot(xa, a2_ref[dd]))

        wpre = w0_ref[d:d + 1, :] + _dot(jnp.tanh(xw).astype(BF16), w2_ref[d])
        logw = -jnp.exp(-_softplus(-wpre) - 0.5)
        yield
        a = in_context_rate(d)
        kkr = k * kk_ref[...]
        kk = kkr * lax.rsqrt(jnp.maximum(_split_dot(kkr * kkr, bd, 2), 1e-24))
        yield
        kd = k * (1.0 + (a - 1.0) * ka_ref[...])
        beta = kk * a
        yield

        if rev:
            a_fwd = in_context_rate(0)
            kd_fwd = k * (1.0 + (a_fwd - 1.0) * ka_ref[...])
            bonus_s[rows, :] = _split_dot(r * (kd_fwd * rk_ref[0:1, :] + kd * rk_ref[1:2, :]), bd, 2) * v
            yield
            xg = conv(3 * D_RW + 2 * LANES, C_RW_PAD)
            gate_s[rows, :] = _dot(jax.nn.sigmoid(xg).astype(BF16), g2_ref[...])
            yield

        logp = _split_dot_left(incl_b, logw)
        ends = [cl * cn if rev else cl * cn + cn - 1 for cl in range(RW_GROUP)]
        logpc = jnp.concatenate(
            [jnp.broadcast_to(logp[e:e + 1, :], (cn, D_RW)) for e in ends], axis=0)
        for cl, e in enumerate(ends):
            cc = g * RW_GROUP + cl
            pc_s[cc * 8:(cc + 1) * 8, :] = jnp.broadcast_to(jnp.exp(logp[e:e + 1, :]), (8, D_RW))
        yield
        at_s[rows, :] = (-kk * jnp.exp(logp - logw)).astype(BF16)
        rt_s[rows, :] = (r * jnp.exp(logp)).astype(BF16)
        yield
        einv = jnp.exp(-logp)
        bt_s[rows, :] = (beta * einv).astype(BF16)
        kt_s[rows, :] = (kd * einv).astype(BF16)
        yield
        efin = jnp.exp(logpc - logp)
        bh_s[rows, :] = (beta * efin).astype(BF16)
        kh_s[rows, :] = (kd * efin).astype(BF16)
        yield

    lane = lax.broadcasted_iota(jnp.int32, (cn, LANES), 1)
    head0 = lane < RW_HD

    def stack(x):
        zero = jnp.zeros_like(x)
        return jnp.concatenate([jnp.where(head0, x, zero), jnp.where(head0, zero, x)], axis=0)

    ri = lax.broadcasted_iota(jnp.int32, (LANES, LANES), 0)
    ci = lax.broadcasted_iota(jnp.int32, (LANES, LANES), 1)
    same_head = (ri < RW_HD) == (ci < RW_HD)
    t2 = lax.broadcasted_iota(jnp.int32, (cn, LANES), 0)
    i2 = lax.broadcasted_iota(jnp.int32, (cn, LANES), 1) & (cn - 1)
    incl2 = (i2 >= t2) if rev else (i2 <= t2)
    strict2 = (i2 > t2) if rev else (i2 < t2)
    eye2 = jnp.where(i2 == t2, 1.0, 0.0)
    zero_bd = jnp.zeros((LANES, LANES), BF16)

    npair = RW_HEADS // 2

    def tile(ref, cc, p):
        return ref[cc * cn:(cc + 1) * cn, p * LANES:(p + 1) * LANES]

    def solve(g):
        g0 = g * RW_GROUP
        probs = [(cc, p) for cc in range(g0, g0 + RW_GROUP) for p in range(npair)]
        ast, vst, rtp, tinv, pw, aak, ark = {}, {}, {}, {}, {}, {}, {}
        for q in probs:
            at_q = tile(at_s, *q)
            ast[q] = stack(at_q)
            vst[q] = stack(tile(v_s, *q))
            rtp[q] = tile(rt_s, *q)
            lhs = jnp.concatenate([at_q, rtp[q]], axis=0)
            rhs = jnp.concatenate([stack(tile(bt_s, *q)), stack(tile(kt_s, *q))], axis=0)
            res = _dot_nt(lhs, rhs)
            pw[q] = jnp.where(strict2, res[:cn, :LANES], 0.0)
            aak[q] = jnp.where(strict2, res[:cn, LANES:], 0.0).astype(BF16)
            ark[q] = jnp.concatenate([jnp.where(incl2, res[cn:, :LANES], 0.0),
                                      jnp.where(incl2, res[cn:, LANES:], 0.0)], axis=1).astype(BF16)
            tinv[q] = eye2 + pw[q]
        yield
        for q in probs:
            pw_b = pw[q].astype(BF16)
            pw[q] = _dot(pw_b, stack(pw_b))
        yield
        for _ in range(4):
            for q in probs:
                pw_b = pw[q].astype(BF16)
                both = _dot(jnp.concatenate([tinv[q].astype(BF16), pw_b], axis=0), stack(pw_b))
                tinv[q] = tinv[q] + both[:cn]
                pw[q] = both[cn:]
            yield
        for q in probs:
            tinv[q] = tinv[q] + _dot(tinv[q].astype(BF16), stack(pw[q].astype(BF16)))
        yield
        wv = {q: _dot(aak[q], vst[q]).astype(BF16) for q in probs}
        yield
        ta = {q: _dot(tinv[q].astype(BF16), jnp.concatenate([ast[q], stack(wv[q])], axis=1)).astype(BF16)
              for q in probs}
        yield
        for q in probs:
            cc, p = q
            ah_q, uh_q = ta[q][:, :LANES], ta[q][:, LANES:]
            rhs = jnp.concatenate([jnp.concatenate([stack(ah_q), stack(uh_q)], axis=1),
                                   jnp.concatenate([zero_bd, vst[q]], axis=1)], axis=0)
            ry = _dot(ark[q], rhs)
            tile_rows = slice(cc * cn, (cc + 1) * cn)
            ls = slice(p * LANES, (p + 1) * LANES)
            rhat_s[tile_rows, ls] = (rtp[q].astype(F32) + ry[:, :LANES]).astype(BF16)
            yhat_s[tile_rows, ls] = ry[:, LANES:]
        yield
        for q in probs:
            cc, p = q
            ah_q, uh_q = ta[q][:, :LANES], ta[q][:, LANES:]
            bh_q = tile(bh_s, *q)
            m_bd = _dot_tn(ah_q, bh_q)
            m_s[cc * npair + p] = jnp.where(same_head, m_bd, 0.0).astype(BF16)
            g_bd = _dot_tn(jnp.concatenate([uh_q, tile(v_s, *q)], axis=0),
                           jnp.concatenate([bh_q, tile(kh_s, *q)], axis=0))
            g_bd = jnp.where(same_head, g_bd, 0.0)
            g_s[cc * npair + p] = g_bd[:RW_HD] + g_bd[RW_HD:]
        yield

    def interleave(*gens):
        live = list(gens)
        while live:
            live = [gen for gen in live if next(gen, live) is not live]

    interleave(prepare(0))
    for g in range(ngroups):
        interleave(solve(g), *([prepare(g + 1)] if g + 1 < ngroups else []))

    s_cur = [state[p] for p in range(npair)]
    for cc in (range(nck - 1, -1, -1) if rev else range(nck)):
        tile_rows = slice(cc * cn, (cc + 1) * cn)
        for p in range(npair):
            ls = slice(p * LANES, (p + 1) * LANES)
            s_b = s_cur[p].astype(BF16)
            y = _dot_nt(rhat_s[tile_rows, ls], stack(s_b)) + yhat_s[tile_rows, ls]
            if rev:
                yhat_s[tile_rows, ls] = y + yf_ref[tile_rows, ls]
            else:
                o_ref[tile_rows, ls] = y
            s_cur[p] = (s_cur[p] * pc_s[cc * 8:cc * 8 + 1, ls]
                        + _dot(s_b, m_s[cc * npair + p]) + g_s[cc * npair + p])
    for p in range(npair):
        state[p] = s_cur[p]

    if not rev:
        return

    ytot = yhat_s[...]
    mu = _split_dot(ytot, bd, 2) * (1.0 / RW_HD)
    yc = ytot - mu
    var = _split_dot(yc * yc, bd, 2) * (1.0 / RW_HD)
    yn = yc * lax.rsqrt(var + GN_EPS) * lnw_ref[...] + lnb_ref[...]
    o_ref[...] = ((yn + bonus_s[...]) * gate_s[...]).astype(BF16)


def _split_dot_left(m, x):
    hi = x.astype(BF16)
    r1 = x - hi.astype(F32)
    mid = r1.astype(BF16)
    lo = (r1 - mid.astype(F32)).astype(BF16)
    return _dot(m, hi) + _dot(m, mid) + _dot(m, lo)


def _rwkv_pass(zrw, yf, wts, batch, seq, rev):
    ts = RW_STEP
    assert seq % ts == 0 and ts % RW_CHUNK == 0 and 2 * RW_CHUNK == LANES
    nch = seq // ts
    nck = ts // RW_CHUNK
    npair = RW_HEADS // 2
    sub = ts // 8
    nblk8 = batch * seq // 8

    def chunk(j):
        return (nch - 1 - j) if rev else j

    main = lambda n: pl.BlockSpec((ts, n), lambda b, j: (b * nch + chunk(j), 0))
    prev = pl.BlockSpec((8, C_RW_PAD), lambda b, j: (jnp.maximum((b * nch + chunk(j)) * sub - 1, 0), 0))
    nxt = pl.BlockSpec((8, C_RW_PAD),
                       lambda b, j: (jnp.minimum((b * nch + chunk(j) + 1) * sub, nblk8 - 1), 0))
    in_specs = [main(C_RW_PAD), prev, nxt] + [_const_spec(w.shape) for w in wts]
    args = [zrw, zrw, zrw] + list(wts)
    scratch = [pltpu.VMEM((npair, RW_HD, LANES), F32)]
    scratch += [pltpu.VMEM((ts, D_RW), BF16) for _ in range(8)]
    scratch += [pltpu.VMEM((ts, D_RW), F32),
                pltpu.VMEM((nck * npair, LANES, LANES), BF16),
                pltpu.VMEM((nck * npair, RW_HD, LANES), F32),
                pltpu.VMEM((nck * 8, D_RW), F32)]
    if rev:
        in_specs.append(main(D_RW))
        args.append(yf)
        scratch += [pltpu.VMEM((ts, D_RW), F32), pltpu.VMEM((ts, D_RW), F32)]
    return pl.pallas_call(
        functools.partial(_rwkv_kernel, rev=rev, nsteps=nch),
        out_shape=jax.ShapeDtypeStruct((batch * seq, D_RW), BF16 if rev else F32),
        grid=(batch, nch),
        in_specs=in_specs,
        out_specs=main(D_RW),
        scratch_shapes=scratch,
        compiler_params=pltpu.CompilerParams(
            dimension_semantics=("parallel", "arbitrary"), vmem_limit_bytes=VMEM_LIMIT),
        name="rwkv_bwd" if rev else "rwkv_fwd",
    )(*args)


def _merge_kernel(x_ref, na_ref, rw_ref, mem_ref, gate_ref, wna_ref, wrw_ref, wmem_ref, wout_ref, o_ref):
    d = D_MODEL
    merged = (gate_ref[:, 0:d].astype(F32) * _dot(na_ref[...], wna_ref[...])
              + gate_ref[:, d:2 * d].astype(F32) * _dot(rw_ref[...], wrw_ref[...])
              + gate_ref[:, 2 * d:3 * d].astype(F32) * _dot(mem_ref[...], wmem_ref[...]))
    o_ref[...] = x_ref[...] + _dot(merged.astype(BF16), wout_ref[...])


def _merge(x, y_na, y_rw, y_mem, gates, wna, wrw, wmem, wout):
    t = x.shape[0]
    tm = TOKEN_TILE
    tok = lambda n: pl.BlockSpec((tm, n), lambda i: (i, 0))
    return pl.pallas_call(
        _merge_kernel,
        out_shape=jax.ShapeDtypeStruct((t, D_MODEL), F32),
        grid=(t // tm,),
        in_specs=[tok(D_MODEL), tok(D_NA), tok(D_RW), tok(D_MEM), tok(C_GATE),
                  _const_spec((D_NA, D_MODEL)), _const_spec((D_RW, D_MODEL)),
                  _const_spec((D_MEM, D_MODEL)), _const_spec((D_MODEL, D_MODEL))],
        out_specs=tok(D_MODEL),
        compiler_params=pltpu.CompilerParams(
            dimension_semantics=("parallel",), vmem_limit_bytes=VMEM_LIMIT),
        name="merge",
    )(x, y_na, y_rw, y_mem, gates, wna, wrw, wmem, wout)


def _block_diag_ones(n, blk):
    idx = np.arange(n) // blk
    return jnp.asarray(idx[:, None] == idx[None, :], dtype=BF16)


def _prep_params(p):
    row = lambda a: a.reshape(1, -1).astype(F32)
    w_in = p['w_in']
    rw_lo, rw_hi = C_NA, C_NA + C_RW
    w_in_p = jnp.concatenate([
        w_in[:, :C_NA],
        jnp.pad(w_in[:, rw_lo:rw_hi], ((0, 0), (0, C_RW_PAD - C_RW))),
        w_in[:, rw_hi:],
    ], axis=1).astype(BF16)

    def lora_stack(w):
        z = jnp.zeros_like(w[0])
        return jnp.stack([jnp.concatenate([w[0], z], axis=0),
                          jnp.concatenate([z, w[1]], axis=0)]).astype(BF16)

    q = {
        'g_ffn1': row(p['g_ffn1']), 'g_ffn2': row(p['g_ffn2']), 'g_mix': row(p['g_mix']),
        'wg1': p['w_ffn1_gate'].astype(BF16), 'wu1': p['w_ffn1_up'].astype(BF16),
        'wd1': p['w_ffn1_down'].astype(BF16),
        'wg2': p['w_ffn2_gate'].astype(BF16), 'wu2': p['w_ffn2_up'].astype(BF16),
        'wd2': p['w_ffn2_down'].astype(BF16),
        'w_in_p': w_in_p,
        'gq': row(jnp.tile(p['g_qn_na'], NA_HEADS)), 'gk': row(jnp.tile(p['g_kn_na'], NA_HEADS)),
        'gmq': row(p['g_qn_mem']), 'gmk': row(p['g_kn_mem']),
        'bd64': _block_diag_ones(D_NA, NA_HD),
        'na_bias': _na_bias_table(p['rpb_na'].astype(F32)),
        'g_mem_norm': row(p['g_mem_norm']), 'w_mem_kv': p['w_mem_kv'].astype(BF16),
        'w_o_na': p['w_o_na'].astype(BF16), 'w_o_rw': p['w_o_rw'].astype(BF16),
        'w_o_mem': p['w_o_mem'].astype(BF16), 'w_out': p['w_out'].astype(BF16),
    }
    q['rw'] = (
        jnp.pad(p['conv_rw'].astype(F32), ((0, 0), (0, C_RW_PAD - C_RW))),
        p['w0_rw'].astype(F32), lora_stack(p['w2_rw']),
        p['a0_rw'].astype(F32), lora_stack(p['a2_rw']),
        jnp.pad(p['g2_rw'], ((0, 2 * LANES - GATE_LORA), (0, 0))).astype(BF16),
        row(p['k_k_rw']), row(p['k_a_rw']), p['r_k_rw'].reshape(2, D_RW).astype(F32),
        row(p['ln_x_w_rw']), row(p['ln_x_b_rw']),
        q['bd64'],
    )
    return q


def _encoder_layer(x, mem, q):
    batch, seq, _ = x.shape
    x2d = x.reshape(batch * seq, D_MODEL)
    x1 = _ffn(x2d, q['g_ffn1'], q['wg1'], q['wu1'], q['wd1'])
    qn, kn, vv, zrw, mq, gates = _proj(x1, q['g_mix'], q['w_in_p'], q['gq'], q['gk'], q['gmq'], q['bd64'])
    y_na = _na(qn, kn, vv, q['na_bias'], batch, seq)
    km, vm = _memkv(mem.reshape(batch * N_MEM, D_MODEL), q['g_mem_norm'], q['w_mem_kv'], q['gmk'])
    y_mem = _memattn(mq, km, vm, seq)
    y_fwd = _rwkv_pass(zrw, None, q['rw'], batch, seq, rev=False)
    y_rw = _rwkv_pass(zrw, y_fwd, q['rw'], batch, seq, rev=True)
    x2 = _merge(x1, y_na, y_rw, y_mem, gates, q['w_o_na'], q['w_o_rw'], q['w_o_mem'], q['w_out'])
    x3 = _ffn(x2, q['g_ffn2'], q['wg2'], q['wu2'], q['wd2'])
    return x3.reshape(batch, seq, D_MODEL)


def kernel(x_prompt, x_sample, mem_prompt, mem_sample, g_ffn1, w_ffn1_gate, w_ffn1_up, w_ffn1_down, g_mix, w_in, g_qn_na, g_kn_na, rpb_na, w_o_na, conv_rw, w0_rw, w2_rw, a0_rw, a2_rw, g2_rw, k_k_rw, k_a_rw, r_k_rw, ln_x_w_rw, ln_x_b_rw, w_o_rw, g_mem_norm, w_mem_kv, g_qn_mem, g_kn_mem, w_o_mem, w_out, g_ffn2, w_ffn2_gate, w_ffn2_up, w_ffn2_down):
    params = {
        'g_ffn1': g_ffn1, 'w_ffn1_gate': w_ffn1_gate, 'w_ffn1_up': w_ffn1_up,
        'w_ffn1_down': w_ffn1_down, 'g_mix': g_mix, 'w_in': w_in,
        'g_qn_na': g_qn_na, 'g_kn_na': g_kn_na, 'rpb_na': rpb_na, 'w_o_na': w_o_na,
        'conv_rw': conv_rw, 'w0_rw': w0_rw, 'w2_rw': w2_rw, 'a0_rw': a0_rw, 'a2_rw': a2_rw,
        'g2_rw': g2_rw, 'k_k_rw': k_k_rw, 'k_a_rw': k_a_rw, 'r_k_rw': r_k_rw,
        'ln_x_w_rw': ln_x_w_rw, 'ln_x_b_rw': ln_x_b_rw, 'w_o_rw': w_o_rw,
        'g_mem_norm': g_mem_norm, 'w_mem_kv': w_mem_kv, 'g_qn_mem': g_qn_mem,
        'g_kn_mem': g_kn_mem, 'w_o_mem': w_o_mem, 'w_out': w_out,
        'g_ffn2': g_ffn2, 'w_ffn2_gate': w_ffn2_gate, 'w_ffn2_up': w_ffn2_up,
        'w_ffn2_down': w_ffn2_down,
    }
    depth = g_ffn1.shape[0]
    y_prompt, y_sample = x_prompt, x_sample
    for layer in range(depth):
        q = _prep_params({name: arr[layer] for name, arr in params.items()})
        y_prompt = _encoder_layer(y_prompt, mem_prompt, q)
        y_sample = _encoder_layer(y_sample, mem_sample, q)
    return (y_prompt, y_sample)
```

```python
import functools

import jax
import jax.numpy as jnp
import numpy as np
from jax import lax
from jax.experimental import pallas as pl
from jax.experimental.pallas import tpu as pltpu

F32 = jnp.float32
BF16 = jnp.bfloat16

D_MODEL = 1024
D_FF = 2816
NORM_EPS = 1e-6
GRID_W = 64
N_MEM = 256
NA_HEADS, NA_HD, NA_WIN_R, NA_WIN_C = 8, 64, 8, 16
D_NA = NA_HEADS * NA_HD
RW_HEADS, RW_HD = 8, 64
D_RW = RW_HEADS * RW_HD
DECAY_LORA, AAA_LORA, GATE_LORA = 64, 64, 160
GN_EPS = 64e-5
MEM_HEADS, MEM_HD = 4, 128
D_MEM = MEM_HEADS * MEM_HD
C_NA = 3 * D_NA
C_RW = 3 * D_RW + 2 * DECAY_LORA + 2 * AAA_LORA + GATE_LORA
C_RW_PAD = 2048
C_GATE = 3 * D_MODEL
N_IN_PAD = C_NA + C_RW_PAD + D_MEM + C_GATE

TOKEN_TILE = 512
FF_CHUNK = 256
NA_ROWS_PER_STEP = 8
NA_ROWS_UNROLL = 2
RW_CHUNK = 64
RW_STEP = 256
RW_GROUP = 4
LANES = 128
VMEM_LIMIT = 56 * 1024 * 1024
NEG_BIG = -1e30


def _const_spec(shape):
    nd = len(shape)
    return pl.BlockSpec(shape, lambda *_: (0,) * nd, pipeline_mode=pl.Buffered(1))


def _dot(a, b):
    return jnp.dot(a, b, preferred_element_type=F32)


def _dot_nt(a, b):
    return lax.dot_general(a, b, (((1,), (1,)), ((), ())), preferred_element_type=F32)


def _dot_tn(a, b):
    return lax.dot_general(a, b, (((0,), (0,)), ((), ())), preferred_element_type=F32)


def _split_dot(x, m, parts):
    acc = None
    rem = x
    for i in range(parts):
        piece = rem.astype(BF16)
        term = _dot(piece, m)
        acc = term if acc is None else acc + term
        if i + 1 < parts:
            rem = rem - piece.astype(F32)
    return acc


def _rmsnorm_rows(x, g):
    ms = jnp.mean(x * x, axis=-1, keepdims=True)
    return x * lax.rsqrt(ms + NORM_EPS) * g


def _ffn_kernel(x_ref, g_ref, wg_ref, wu_ref, wd_ref, o_ref, a_scr):
    x = x_ref[...]
    h = _rmsnorm_rows(x, g_ref[...]).astype(BF16)
    for f in range(0, D_FF, FF_CHUNK):
        gate = _dot(h, wg_ref[:, f:f + FF_CHUNK])
        up = _dot(h, wu_ref[:, f:f + FF_CHUNK])
        a_scr[:, f:f + FF_CHUNK] = (gate * jax.nn.sigmoid(gate) * up).astype(BF16)
    o_ref[...] = x + 0.5 * _dot(a_scr[...], wd_ref[...])


def _ffn(x, g, wg, wu, wd):
    t = x.shape[0]
    tm = TOKEN_TILE
    return pl.pallas_call(
        _ffn_kernel,
        out_shape=jax.ShapeDtypeStruct((t, D_MODEL), F32),
        grid=(t // tm,),
        in_specs=[
            pl.BlockSpec((tm, D_MODEL), lambda i: (i, 0)),
            _const_spec((1, D_MODEL)),
            _const_spec((D_MODEL, D_FF)),
            _const_spec((D_MODEL, D_FF)),
            _const_spec((D_FF, D_MODEL)),
        ],
        out_specs=pl.BlockSpec((tm, D_MODEL), lambda i: (i, 0)),
        scratch_shapes=[pltpu.VMEM((tm, D_FF), BF16)],
        compiler_params=pltpu.CompilerParams(
            dimension_semantics=("parallel",), vmem_limit_bytes=VMEM_LIMIT),
        name="ffn",
    )(x, g, wg, wu, wd)


def _proj_kernel(x_ref, xp_ref, xn_ref, g_ref, w_ref, gq_ref, gk_ref, gmq_ref, bd_ref, conv_ref,
                 q_ref, k_ref, v_ref, zrw_ref, mq_ref, gate_ref, *, tiles_per_seq):
    tm = TOKEN_TILE
    h_all = _rmsnorm_rows(jnp.concatenate([x_ref[...], xp_ref[...], xn_ref[...]], axis=0),
                          g_ref[...]).astype(BF16)
    h = h_all[:tm]
    bd = bd_ref[...]
    i = pl.program_id(0)
    pos = i % tiles_per_seq
    has_prev = (pos > 0).astype(F32)
    has_next = (pos < tiles_per_seq - 1).astype(F32)
    row8 = lax.broadcasted_iota(jnp.int32, (8, 1), 0)

    def head_norm(z, gain):
        ms = _split_dot(z * z, bd, 2) * (1.0 / NA_HD)
        return z * lax.rsqrt(ms + NORM_EPS) * gain

    zq = _dot(h, w_ref[:, 0:D_NA])
    q_ref[...] = (head_norm(zq, gq_ref[...]) * (NA_HD ** -0.5)).astype(BF16)
    zk = _dot(h, w_ref[:, D_NA:2 * D_NA])
    k_ref[...] = head_norm(zk, gk_ref[...]).astype(BF16)
    v_ref[...] = _dot(h, w_ref[:, 2 * D_NA:3 * D_NA]).astype(BF16)

    chunks = list(range(0, C_RW_PAD, 512))
    z_next = _dot(h_all, w_ref[:, C_NA:C_NA + 512])
    for n, c in enumerate(chunks):
        z_all = z_next
        if n + 1 < len(chunks):
            z_next = _dot(h_all, w_ref[:, C_NA + c + 512:C_NA + c + 1024])
        z = z_all[:tm]
        before = z_all[tm + 7:tm + 8] * has_prev
        after = z_all[tm + 8:tm + 9] * has_next
        down = pltpu.roll(z, 1, 0)
        up = pltpu.roll(z, tm - 1, 0)
        zm1 = jnp.concatenate([jnp.where(row8 == 0, before, down[:8]), down[8:]], axis=0)
        zp1 = jnp.concatenate([up[:tm - 8], jnp.where(row8 == 7, after, up[tm - 8:])], axis=0)
        zrw_ref[:, c:c + 512] = (zm1 * conv_ref[0:1, c:c + 512] + z * conv_ref[1:2, c:c + 512]
                                 + zp1 * conv_ref[2:3, c:c + 512])

    off = C_NA + C_RW_PAD
    zm = _dot(h, w_ref[:, off:off + D_MEM])
    gmq = gmq_ref[...]
    for hd in range(MEM_HEADS):
        sl = slice(hd * MEM_HD, (hd + 1) * MEM_HD)
        mq_ref[:, sl] = (_rmsnorm_rows(zm[:, sl], gmq) * (MEM_HD ** -0.5)).astype(BF16)

    off += D_MEM
    for c in range(0, C_GATE, 512):
        zg = _dot(h, w_ref[:, off + c:off + c + 512])
        gate_ref[:, c:c + 512] = jax.nn.sigmoid(zg).astype(BF16)


def _proj(x, g, w_in_p, gq, gk, gmq, bd64, conv_w, seq):
    t = x.shape[0]
    tm = TOKEN_TILE
    assert seq % tm == 0
    sub = tm // 8
    nblk8 = t // 8
    tok = lambda n: pl.BlockSpec((tm, n), lambda i: (i, 0))
    prev = pl.BlockSpec((8, D_MODEL), lambda i: (jnp.maximum(i * sub - 1, 0), 0))
    nxt = pl.BlockSpec((8, D_MODEL), lambda i: (jnp.minimum((i + 1) * sub, nblk8 - 1), 0))
    return pl.pallas_call(
        functools.partial(_proj_kernel, tiles_per_seq=seq // tm),
        out_shape=(
            jax.ShapeDtypeStruct((t, D_NA), BF16),
            jax.ShapeDtypeStruct((t, D_NA), BF16),
            jax.ShapeDtypeStruct((t, D_NA), BF16),
            jax.ShapeDtypeStruct((t, C_RW_PAD), F32),
            jax.ShapeDtypeStruct((t, D_MEM), BF16),
            jax.ShapeDtypeStruct((t, C_GATE), BF16),
        ),
        grid=(t // tm,),
        in_specs=[
            tok(D_MODEL), prev, nxt,
            _const_spec((1, D_MODEL)),
            _const_spec((D_MODEL, N_IN_PAD)),
            _const_spec((1, D_NA)),
            _const_spec((1, D_NA)),
            _const_spec((1, MEM_HD)),
            _const_spec((D_NA, D_NA)),
            _const_spec((3, C_RW_PAD)),
        ],
        out_specs=(tok(D_NA), tok(D_NA), tok(D_NA), tok(C_RW_PAD), tok(D_MEM), tok(C_GATE)),
        compiler_params=pltpu.CompilerParams(
            dimension_semantics=("parallel",), vmem_limit_bytes=VMEM_LIMIT),
        name="proj",
    )(x, x, x, g, w_in_p, gq, gk, gmq, bd64, conv_w)


def _na_kernel(q_ref, k_ref, v_ref, bias_ref, o_ref, *, rows):
    j = pl.program_id(1)
    lane = lax.broadcasted_iota(jnp.int32, (GRID_W, LANES), 1)
    head0 = lane < NA_HD
    nkeys = NA_WIN_R * GRID_W

    def rows_body(it, carry):
        probs = []
        for u in range(NA_ROWS_UNROLL):
            rr = it * NA_ROWS_UNROLL + u
            r = j * NA_ROWS_PER_STEP + rr
            r0 = jnp.clip(r - NA_WIN_R // 2, 0, rows - NA_WIN_R)
            qoff = pl.multiple_of(rr * GRID_W, GRID_W)
            koff = pl.multiple_of(r0 * GRID_W, GRID_W)
            for p in range(NA_HEADS // 2):
                probs.append((r - r0, qoff, koff, p, slice(p * LANES, (p + 1) * LANES)))
        scores = []
        for d, qoff, koff, p, ls in probs:
            qp = q_ref[pl.ds(qoff, GRID_W), ls]
            zero = jnp.zeros_like(qp)
            qst = jnp.concatenate([jnp.where(head0, qp, zero), jnp.where(head0, zero, qp)], axis=0)
            scores.append(_dot_nt(qst, k_ref[pl.ds(koff, nkeys), ls]) + bias_ref[d, p])
        weights, inv_sums = [], []
        for s in scores:
            e = jnp.exp(s - jnp.max(s, axis=-1, keepdims=True))
            inv_sums.append(1.0 / jnp.sum(e, axis=-1, keepdims=True))
            weights.append(e.astype(BF16))
        outs = [_dot(w, v_ref[pl.ds(koff, nkeys), ls]) * inv
                for w, inv, (d, qoff, koff, p, ls) in zip(weights, inv_sums, probs)]
        for o, (d, qoff, koff, p, ls) in zip(outs, probs):
            o_ref[pl.ds(qoff, GRID_W), ls] = jnp.where(head0, o[:GRID_W], o[GRID_W:]).astype(BF16)
        return carry

    lax.fori_loop(0, NA_ROWS_PER_STEP // NA_ROWS_UNROLL, rows_body, 0)


def _na(q, k, v, bias, batch, seq):
    rows = seq // GRID_W
    assert rows >= NA_WIN_R and rows % NA_ROWS_PER_STEP == 0
    steps = rows // NA_ROWS_PER_STEP
    tq = NA_ROWS_PER_STEP * GRID_W
    return pl.pallas_call(
        functools.partial(_na_kernel, rows=rows),
        out_shape=jax.ShapeDtypeStruct((batch * seq, D_NA), BF16),
        grid=(batch, steps),
        in_specs=[
            pl.BlockSpec((tq, D_NA), lambda b, j: (b * steps + j, 0)),
            pl.BlockSpec((seq, D_NA), lambda b, j: (b, 0)),
            pl.BlockSpec((seq, D_NA), lambda b, j: (b, 0)),
            _const_spec(bias.shape),
        ],
        out_specs=pl.BlockSpec((tq, D_NA), lambda b, j: (b * steps + j, 0)),
        compiler_params=pltpu.CompilerParams(
            dimension_semantics=("parallel", "arbitrary"), vmem_limit_bytes=VMEM_LIMIT),
        name="natten",
    )(q, k, v, bias)


def _na_bias_table(rpb):
    wc = NA_WIN_C
    cols = np.arange(GRID_W)
    col_start = np.clip(cols - wc // 2, 0, GRID_W - wc)
    cc = np.arange(GRID_W)
    valid = (cc[None, :] >= col_start[:, None]) & (cc[None, :] < col_start[:, None] + wc)
    padded = jnp.pad(rpb, ((0, 0), (0, 0), (GRID_W - 1, GRID_W - 1)))
    by_col = jnp.stack([padded[:, :, GRID_W + NA_WIN_C - 2 - c0:2 * GRID_W + NA_WIN_C - 2 - c0]
                        for c0 in range(GRID_W)], axis=2)
    tab = jnp.stack([by_col[:, NA_WIN_R - 1 - d0:2 * NA_WIN_R - 1 - d0] for d0 in range(NA_WIN_R)],
                    axis=1)
    tab = jnp.where(valid[None, None, None], tab, NEG_BIG)
    tab = tab.transpose(1, 0, 3, 2, 4).reshape(NA_WIN_R, NA_HEADS // 2, 2 * GRID_W, NA_WIN_R * GRID_W)
    return tab.astype(F32)


def _memkv_kernel(m_ref, g_ref, w_ref, gk_ref, km_ref, vm_ref):
    h = _rmsnorm_rows(m_ref[...], g_ref[...]).astype(BF16)
    kv = _dot(h, w_ref[...])
    gk = gk_ref[...]
    for hd in range(MEM_HEADS):
        sl = slice(hd * MEM_HD, (hd + 1) * MEM_HD)
        km_ref[:, sl] = _rmsnorm_rows(kv[:, sl], gk).astype(BF16)
    vm_ref[...] = kv[:, D_MEM:].astype(BF16)


def _memkv(mem2d, g, w, gk):
    n = mem2d.shape[0]
    blk = pl.BlockSpec((N_MEM, D_MEM), lambda i: (i, 0))
    return pl.pallas_call(
        _memkv_kernel,
        out_shape=(jax.ShapeDtypeStruct((n, D_MEM), BF16), jax.ShapeDtypeStruct((n, D_MEM), BF16)),
        grid=(n // N_MEM,),
        in_specs=[
            pl.BlockSpec((N_MEM, D_MODEL), lambda i: (i, 0)),
            _const_spec((1, D_MODEL)),
            _const_spec((D_MODEL, 2 * D_MEM)),
            _const_spec((1, MEM_HD)),
        ],
        out_specs=(blk, blk),
        compiler_params=pltpu.CompilerParams(
            dimension_semantics=("parallel",), vmem_limit_bytes=VMEM_LIMIT),
        name="memkv",
    )(mem2d, g, w, gk)


def _memattn_kernel(q_ref, km_ref, vm_ref, o_ref):
    for hd in range(MEM_HEADS):
        sl = slice(hd * MEM_HD, (hd + 1) * MEM_HD)
        s = _dot_nt(q_ref[:, sl], km_ref[:, sl])
        m = jnp.max(s, axis=-1, keepdims=True)
        e = jnp.exp(s - m)
        inv = 1.0 / jnp.sum(e, axis=-1, keepdims=True)
        o_ref[:, sl] = (_dot(e.astype(BF16), vm_ref[:, sl]) * inv).astype(BF16)


def _memattn(q, km, vm, seq):
    t = q.shape[0]
    tm = TOKEN_TILE
    assert seq % tm == 0
    mem_blk = pl.BlockSpec((N_MEM, D_MEM), lambda i: ((i * tm) // seq, 0))
    return pl.pallas_call(
        _memattn_kernel,
        out_shape=jax.ShapeDtypeStruct((t, D_MEM), BF16),
        grid=(t // tm,),
        in_specs=[pl.BlockSpec((tm, D_MEM), lambda i: (i, 0)), mem_blk, mem_blk],
        out_specs=pl.BlockSpec((tm, D_MEM), lambda i: (i, 0)),
        compiler_params=pltpu.CompilerParams(
            dimension_semantics=("parallel",), vmem_limit_bytes=VMEM_LIMIT),
        name="memattn",
    )(q, km, vm)


def _softplus(x):
    return jnp.maximum(x, 0.0) + jnp.log(1.0 + jnp.exp(-jnp.abs(x)))


def _rwkv_kernel(z_ref, w0_ref, w2_ref, a0_ref, a2_ref, g2_ref,
                 kk_ref, ka_ref, rk_ref, lnw_ref, lnb_ref, bd_ref, *rest, rev, nsteps):
    if rev:
        yf_ref, o_ref, *scr = rest
    else:
        o_ref, *scr = rest
    state, rhat_s, yhat_s, m_s, g_s = scr[:5]
    nstaged = (len(scr) - 5) // 2
    staged_cur = scr[5:5 + nstaged]
    staged_nxt = scr[5 + nstaged:]
    cn = RW_CHUNK
    ts = RW_STEP
    nck = ts // cn
    j = pl.program_id(1)
    d = 1 if rev else 0
    at_s, rt_s, bt_s, kt_s, bh_s, kh_s, v_s, pc_s = staged_cur[:8]
    at_w, rt_w, bt_w, kt_w, bh_w, kh_w, v_w, pc_w = staged_nxt[:8]
    if rev:
        bonus_s, gate_s = staged_cur[8:]
        bonus_w, gate_w = staged_nxt[8:]

    @pl.when(j == 0)
    def _():
        state[...] = jnp.zeros_like(state)
        for s in staged_cur:
            s[...] = jnp.zeros_like(s)

    gr = RW_GROUP * cn
    ngroups = ts // gr
    bd = bd_ref[...]
    shift = cn.bit_length() - 1
    ti = lax.broadcasted_iota(jnp.int32, (gr, gr), 0)
    ii = lax.broadcasted_iota(jnp.int32, (gr, gr), 1)
    incl = ((ti >> shift) == (ii >> shift)) & ((ii >= ti) if rev else (ii <= ti))
    incl_b = jnp.where(incl, 1.0, 0.0).astype(BF16)

    def prepare(g):
        r0 = g * gr
        rows = slice(r0, r0 + gr)

        def conv(lo, hi):
            return z_ref[rows, lo:hi]

        r = conv(0, D_RW)
        yield
        k = conv(D_RW, 2 * D_RW)
        yield
        v = conv(2 * D_RW, 3 * D_RW)
        v_w[rows, :] = v.astype(BF16)
        yield
        xw = conv(3 * D_RW, 3 * D_RW + LANES)
        xa = conv(3 * D_RW + LANES, 3 * D_RW + 2 * LANES).astype(BF16)

        def in_context_rate(dd):
            return jax.nn.sigmoid(a0_ref[dd:dd + 1, :] + _dot(xa, a2_ref[dd]))

        wpre = w0_ref[d:d + 1, :] + _dot(jnp.tanh(xw).astype(BF16), w2_ref[d])
        logw = -jnp.exp(-_softplus(-wpre) - 0.5)
        yield
        a = in_context_rate(d)
        kkr = k * kk_ref[...]
        kk = kkr * lax.rsqrt(jnp.maximum(_split_dot(kkr * kkr, bd, 2), 1e-24))
        yield
        kd = k * (1.0 + (a - 1.0) * ka_ref[...])
        beta = kk * a
        yield

        if rev:
            a_fwd = in_context_rate(0)
            kd_fwd = k * (1.0 + (a_fwd - 1.0) * ka_ref[...])
            bonus_w[rows, :] = _split_dot(r * (kd_fwd * rk_ref[0:1, :] + kd * rk_ref[1:2, :]), bd, 2) * v
            yield
            xg = conv(3 * D_RW + 2 * LANES, C_RW_PAD)
            gate_w[rows, :] = _dot(jax.nn.sigmoid(xg).astype(BF16), g2_ref[...])
            yield

        logp = _split_dot_left(incl_b, logw)
        ends = [cl * cn if rev else cl * cn + cn - 1 for cl in range(RW_GROUP)]
        logpc = jnp.concatenate(
            [jnp.broadcast_to(logp[e:e + 1, :], (cn, D_RW)) for e in ends], axis=0)
        for cl, e in enumerate(ends):
            cc = g * RW_GROUP + cl
            pc_w[cc * 8:(cc + 1) * 8, :] = jnp.broadcast_to(jnp.exp(logp[e:e + 1, :]), (8, D_RW))
        yield
        at_w[rows, :] = (-kk * jnp.exp(logp - logw)).astype(BF16)
        rt_w[rows, :] = (r * jnp.exp(logp)).astype(BF16)
        yield
        einv = jnp.exp(-logp)
        bt_w[rows, :] = (beta * einv).astype(BF16)
        kt_w[rows, :] = (kd * einv).astype(BF16)
        yield
        efin = jnp.exp(logpc - logp)
        bh_w[rows, :] = (beta * efin).astype(BF16)
        kh_w[rows, :] = (kd * efin).astype(BF16)
        yield

    lane = lax.broadcasted_iota(jnp.int32, (cn, LANES), 1)
    head0 = lane < RW_HD

    def stack(x):
        zero = jnp.zeros_like(x)
        return jnp.concatenate([jnp.where(head0, x, zero), jnp.where(head0, zero, x)], axis=0)

    ri = lax.broadcasted_iota(jnp.int32, (LANES, LANES), 0)
    ci = lax.broadcasted_iota(jnp.int32, (LANES, LANES), 1)
    same_head = (ri < RW_HD) == (ci < RW_HD)
    t2 = lax.broadcasted_iota(jnp.int32, (cn, LANES), 0)
    i2 = lax.broadcasted_iota(jnp.int32, (cn, LANES), 1) & (cn - 1)
    incl2 = (i2 >= t2) if rev else (i2 <= t2)
    strict2 = (i2 > t2) if rev else (i2 < t2)
    eye2 = jnp.where(i2 == t2, 1.0, 0.0)
    zero_bd = jnp.zeros((LANES, LANES), BF16)

    npair = RW_HEADS // 2

    def tile(ref, cc, p):
        return ref[cc * cn:(cc + 1) * cn, p * LANES:(p + 1) * LANES]

    def solve(g):
        g0 = g * RW_GROUP
        probs = [(cc, p) for cc in range(g0, g0 + RW_GROUP) for p in range(npair)]
        ast, vst, rtp, tinv, pw, aak, ark = {}, {}, {}, {}, {}, {}, {}
        for q in probs:
            at_q = tile(at_s, *q)
            ast[q] = stack(at_q)
            vst[q] = stack(tile(v_s, *q))
            rtp[q] = tile(rt_s, *q)
            lhs = jnp.concatenate([at_q, rtp[q]], axis=0)
            rhs = jnp.concatenate([stack(tile(bt_s, *q)), stack(tile(kt_s, *q))], axis=0)
            res = _dot_nt(lhs, rhs)
            pw[q] = jnp.where(strict2, res[:cn, :LANES], 0.0)
            aak[q] = jnp.where(strict2, res[:cn, LANES:], 0.0).astype(BF16)
            ark[q] = jnp.concatenate([jnp.where(incl2, res[cn:, :LANES], 0.0),
                                      jnp.where(incl2, res[cn:, LANES:], 0.0)], axis=1).astype(BF16)
            tinv[q] = eye2 + pw[q]
        yield
        for q in probs:
            pw_b = pw[q].astype(BF16)
            pw[q] = _dot(pw_b, stack(pw_b))
        yield
        for _ in range(4):
            for q in probs:
                pw_b = pw[q].astype(BF16)
                both = _dot(jnp.concatenate([tinv[q].astype(BF16), pw_b], axis=0), stack(pw_b))
                tinv[q] = tinv[q] + both[:cn]
                pw[q] = both[cn:]
            yield
        for q in probs:
            tinv[q] = tinv[q] + _dot(tinv[q].astype(BF16), stack(pw[q].astype(BF16)))
        yield
        wv = {q: _dot(aak[q], vst[q]).astype(BF16) for q in probs}
        yield
        ta = {q: _dot(tinv[q].astype(BF16), jnp.concatenate([ast[q], stack(wv[q])], axis=1)).astype(BF16)
              for q in probs}
        yield
        for q in probs:
            cc, p = q
            ah_q, uh_q = ta[q][:, :LANES], ta[q][:, LANES:]
            rhs = jnp.concatenate([jnp.concatenate([stack(ah_q), stack(uh_q)], axis=1),
                                   jnp.concatenate([zero_bd, vst[q]], axis=1)], axis=0)
            ry = _dot(ark[q], rhs)
            tile_rows = slice(cc * cn, (cc + 1) * cn)
            ls = slice(p * LANES, (p + 1) * LANES)
            rhat_s[tile_rows, ls] = (rtp[q].astype(F32) + ry[:, :LANES]).astype(BF16)
            yhat_s[tile_rows, ls] = ry[:, LANES:]
        yield
        for q in probs:
            cc, p = q
            ah_q, uh_q = ta[q][:, :LANES], ta[q][:, LANES:]
            bh_q = tile(bh_s, *q)
            m_bd = _dot_tn(ah_q, bh_q)
            m_s[cc * npair + p] = jnp.where(same_head, m_bd, 0.0).astype(BF16)
            g_bd = _dot_tn(jnp.concatenate([uh_q, tile(v_s, *q)], axis=0),
                           jnp.concatenate([bh_q, tile(kh_s, *q)], axis=0))
            g_bd = jnp.where(same_head, g_bd, 0.0)
            g_s[cc * npair + p] = g_bd[:RW_HD] + g_bd[RW_HD:]
        yield

    def recur():
        s_cur = [state[p] for p in range(npair)]
        for cc in (range(nck - 1, -1, -1) if rev else range(nck)):
            tile_rows = slice(cc * cn, (cc + 1) * cn)
            for p in range(npair):
                ls = slice(p * LANES, (p + 1) * LANES)
                s_b = s_cur[p].astype(BF16)
                y = _dot_nt(rhat_s[tile_rows, ls], stack(s_b)) + yhat_s[tile_rows, ls]
                if rev:
                    yhat_s[tile_rows, ls] = y + yf_ref[tile_rows, ls]
                else:
                    o_ref[tile_rows, ls] = y
                s_cur[p] = (s_cur[p] * pc_s[cc * 8:cc * 8 + 1, ls]
                            + _dot(s_b, m_s[cc * npair + p]) + g_s[cc * npair + p])
            yield
        for p in range(npair):
            state[p] = s_cur[p]
        if rev:
            for g in range(ngroups):
                rows = slice(g * gr, (g + 1) * gr)
                ytot = yhat_s[rows, :]
                mu = _split_dot(ytot, bd, 2) * (1.0 / RW_HD)
                yc = ytot - mu
                var = _split_dot(yc * yc, bd, 2) * (1.0 / RW_HD)
                yn = yc * lax.rsqrt(var + GN_EPS) * lnw_ref[...] + lnb_ref[...]
                o_ref[rows, :] = ((yn + bonus_s[rows, :]) * gate_s[rows, :]).astype(BF16)
                yield

    def chain(*gens):
        for gen in gens:
            yield from gen

    def interleave(*gens):
        live = list(gens)
        while live:
            live = [gen for gen in live if next(gen, live) is not live]

    interleave(chain(*[solve(g) for g in range(ngroups)], recur()),
               chain(*[prepare(g) for g in range(ngroups)]))
    for cur, nxt in zip(staged_cur, staged_nxt):
        cur[...] = nxt[...]


def _split_dot_left(m, x):
    hi = x.astype(BF16)
    r1 = x - hi.astype(F32)
    mid = r1.astype(BF16)
    lo = (r1 - mid.astype(F32)).astype(BF16)
    return _dot(m, hi) + _dot(m, mid) + _dot(m, lo)


def _rwkv_pass(zrw, yf, wts, batch, seq, rev):
    ts = RW_STEP
    assert seq % ts == 0 and ts % RW_CHUNK == 0 and 2 * RW_CHUNK == LANES
    nch = seq // ts
    nck = ts // RW_CHUNK
    npair = RW_HEADS // 2

    def block(step):
        return (nch - 1 - step) if rev else step

    def prep_blk(b, j):
        return b * nch + block(jnp.minimum(j, nch - 1))

    def solve_blk(b, j):
        return b * nch + block(jnp.maximum(j - 1, 0))

    zmain = pl.BlockSpec((ts, C_RW_PAD), lambda b, j: (prep_blk(b, j), 0))
    ymain = pl.BlockSpec((ts, D_RW), lambda b, j: (solve_blk(b, j), 0))
    in_specs = [zmain] + [_const_spec(w.shape) for w in wts]
    args = [zrw] + list(wts)
    scratch = [pltpu.VMEM((npair, RW_HD, LANES), F32),
               pltpu.VMEM((ts, D_RW), BF16),
               pltpu.VMEM((ts, D_RW), F32),
               pltpu.VMEM((nck * npair, LANES, LANES), BF16),
               pltpu.VMEM((nck * npair, RW_HD, LANES), F32)]
    staged = [pltpu.VMEM((ts, D_RW), BF16) for _ in range(7)]
    staged += [pltpu.VMEM((nck * 8, D_RW), F32)]
    if rev:
        in_specs.append(ymain)
        args.append(yf)
        staged += [pltpu.VMEM((ts, D_RW), F32), pltpu.VMEM((ts, D_RW), F32)]
    scratch += staged + staged
    return pl.pallas_call(
        functools.partial(_rwkv_kernel, rev=rev, nsteps=nch),
        out_shape=jax.ShapeDtypeStruct((batch * seq, D_RW), BF16 if rev else F32),
        grid=(batch, nch + 1),
        in_specs=in_specs,
        out_specs=ymain,
        scratch_shapes=scratch,
        compiler_params=pltpu.CompilerParams(
            dimension_semantics=("parallel", "arbitrary"), vmem_limit_bytes=VMEM_LIMIT),
        name="rwkv_bwd" if rev else "rwkv_fwd",
    )(*args)


def _merge_kernel(x_ref, na_ref, rw_ref, mem_ref, gate_ref, wna_ref, wrw_ref, wmem_ref, wout_ref, o_ref):
    d = D_MODEL
    merged = (gate_ref[:, 0:d].astype(F32) * _dot(na_ref[...], wna_ref[...])
              + gate_ref[:, d:2 * d].astype(F32) * _dot(rw_ref[...], wrw_ref[...])
              + gate_ref[:, 2 * d:3 * d].astype(F32) * _dot(mem_ref[...], wmem_ref[...]))
    o_ref[...] = x_ref[...] + _dot(merged.astype(BF16), wout_ref[...])


def _merge(x, y_na, y_rw, y_mem, gates, wna, wrw, wmem, wout):
    t = x.shape[0]
    tm = TOKEN_TILE
    tok = lambda n: pl.BlockSpec((tm, n), lambda i: (i, 0))
    return pl.pallas_call(
        _merge_kernel,
        out_shape=jax.ShapeDtypeStruct((t, D_MODEL), F32),
        grid=(t // tm,),
        in_specs=[tok(D_MODEL), tok(D_NA), tok(D_RW), tok(D_MEM), tok(C_GATE),
                  _const_spec((D_NA, D_MODEL)), _const_spec((D_RW, D_MODEL)),
                  _const_spec((D_MEM, D_MODEL)), _const_spec((D_MODEL, D_MODEL))],
        out_specs=tok(D_MODEL),
        compiler_params=pltpu.CompilerParams(
            dimension_semantics=("parallel",), vmem_limit_bytes=VMEM_LIMIT),
        name="merge",
    )(x, y_na, y_rw, y_mem, gates, wna, wrw, wmem, wout)


def _block_diag_ones(n, blk):
    idx = np.arange(n) // blk
    return jnp.asarray(idx[:, None] == idx[None, :], dtype=BF16)


def _prep_params(p):
    row = lambda a: a.reshape(1, -1).astype(F32)
    w_in = p['w_in']
    rw_lo, rw_hi = C_NA, C_NA + C_RW
    w_in_p = jnp.concatenate([
        w_in[:, :C_NA],
        jnp.pad(w_in[:, rw_lo:rw_hi], ((0, 0), (0, C_RW_PAD - C_RW))),
        w_in[:, rw_hi:],
    ], axis=1).astype(BF16)

    def lora_stack(w):
        z = jnp.zeros_like(w[0])
        return jnp.stack([jnp.concatenate([w[0], z], axis=0),
                          jnp.concatenate([z, w[1]], axis=0)]).astype(BF16)

    q = {
        'g_ffn1': row(p['g_ffn1']), 'g_ffn2': row(p['g_ffn2']), 'g_mix': row(p['g_mix']),
        'wg1': p['w_ffn1_gate'].astype(BF16), 'wu1': p['w_ffn1_up'].astype(BF16),
        'wd1': p['w_ffn1_down'].astype(BF16),
        'wg2': p['w_ffn2_gate'].astype(BF16), 'wu2': p['w_ffn2_up'].astype(BF16),
        'wd2': p['w_ffn2_down'].astype(BF16),
        'w_in_p': w_in_p,
        'gq': row(jnp.tile(p['g_qn_na'], NA_HEADS)), 'gk': row(jnp.tile(p['g_kn_na'], NA_HEADS)),
        'gmq': row(p['g_qn_mem']), 'gmk': row(p['g_kn_mem']),
        'bd64': _block_diag_ones(D_NA, NA_HD),
        'na_bias': _na_bias_table(p['rpb_na'].astype(F32)),
        'g_mem_norm': row(p['g_mem_norm']), 'w_mem_kv': p['w_mem_kv'].astype(BF16),
        'w_o_na': p['w_o_na'].astype(BF16), 'w_o_rw': p['w_o_rw'].astype(BF16),
        'w_o_mem': p['w_o_mem'].astype(BF16), 'w_out': p['w_out'].astype(BF16),
    }
    q['conv'] = jnp.pad(p['conv_rw'].astype(F32), ((0, 0), (0, C_RW_PAD - C_RW)))
    q['rw'] = (
        p['w0_rw'].astype(F32), lora_stack(p['w2_rw']),
        p['a0_rw'].astype(F32), lora_stack(p['a2_rw']),
        jnp.pad(p['g2_rw'], ((0, 2 * LANES - GATE_LORA), (0, 0))).astype(BF16),
        row(p['k_k_rw']), row(p['k_a_rw']), p['r_k_rw'].reshape(2, D_RW).astype(F32),
        row(p['ln_x_w_rw']), row(p['ln_x_b_rw']),
        q['bd64'],
    )
    return q


def _encoder_layer(x, mem, q):
    batch, seq, _ = x.shape
    x2d = x.reshape(batch * seq, D_MODEL)
    x1 = _ffn(x2d, q['g_ffn1'], q['wg1'], q['wu1'], q['wd1'])
    qn, kn, vv, zrw, mq, gates = _proj(x1, q['g_mix'], q['w_in_p'], q['gq'], q['gk'], q['gmq'], q['bd64'],
                                       q['conv'], seq)
    y_na = _na(qn, kn, vv, q['na_bias'], batch, seq)
    km, vm = _memkv(mem.reshape(batch * N_MEM, D_MODEL), q['g_mem_norm'], q['w_mem_kv'], q['gmk'])
    y_mem = _memattn(mq, km, vm, seq)
    y_fwd = _rwkv_pass(zrw, None, q['rw'], batch, seq, rev=False)
    y_rw = _rwkv_pass(zrw, y_fwd, q['rw'], batch, seq, rev=True)
    x2 = _merge(x1, y_na, y_rw, y_mem, gates, q['w_o_na'], q['w_o_rw'], q['w_o_mem'], q['w_out'])
    x3 = _ffn(x2, q['g_ffn2'], q['wg2'], q['wu2'], q['wd2'])
    return x3.reshape(batch, seq, D_MODEL)


def kernel(x_prompt, x_sample, mem_prompt, mem_sample, g_ffn1, w_ffn1_gate, w_ffn1_up, w_ffn1_down, g_mix, w_in, g_qn_na, g_kn_na, rpb_na, w_o_na, conv_rw, w0_rw, w2_rw, a0_rw, a2_rw, g2_rw, k_k_rw, k_a_rw, r_k_rw, ln_x_w_rw, ln_x_b_rw, w_o_rw, g_mem_norm, w_mem_kv, g_qn_mem, g_kn_mem, w_o_mem, w_out, g_ffn2, w_ffn2_gate, w_ffn2_up, w_ffn2_down):
    params = {
        'g_ffn1': g_ffn1, 'w_ffn1_gate': w_ffn1_gate, 'w_ffn1_up': w_ffn1_up,
        'w_ffn1_down': w_ffn1_down, 'g_mix': g_mix, 'w_in': w_in,
        'g_qn_na': g_qn_na, 'g_kn_na': g_kn_na, 'rpb_na': rpb_na, 'w_o_na': w_o_na,
        'conv_rw': conv_rw, 'w0_rw': w0_rw, 'w2_rw': w2_rw, 'a0_rw': a0_rw, 'a2_rw': a2_rw,
        'g2_rw': g2_rw, 'k_k_rw': k_k_rw, 'k_a_rw': k_a_rw, 'r_k_rw': r_k_rw,
        'ln_x_w_rw': ln_x_w_rw, 'ln_x_b_rw': ln_x_b_rw, 'w_o_rw': w_o_rw,
        'g_mem_norm': g_mem_norm, 'w_mem_kv': w_mem_kv, 'g_qn_mem': g_qn_mem,
        'g_kn_mem': g_kn_mem, 'w_o_mem': w_o_mem, 'w_out': w_out,
        'g_ffn2': g_ffn2, 'w_ffn2_gate': w_ffn2_gate, 'w_ffn2_up': w_ffn2_up,
        'w_ffn2_down': w_ffn2_down,
    }
    depth = g_ffn1.shape[0]
    y_prompt, y_sample = x_prompt, x_sample
    for layer in range(depth):
        q = _prep_params({name: arr[layer] for name, arr in params.items()})
        y_prompt = _encoder_layer(y_prompt, mem_prompt, q)
        y_sample = _encoder_layer(y_sample, mem_sample, q)
    return (y_prompt, y_sample)
```

```python
import functools

import jax
import jax.numpy as jnp
import numpy as np
from jax import lax
from jax.experimental import pallas as pl
from jax.experimental.pallas import tpu as pltpu

F32 = jnp.float32
BF16 = jnp.bfloat16

D_MODEL = 1024
D_FF = 2816
NORM_EPS = 1e-6
GRID_W = 64
N_MEM = 256
NA_HEADS, NA_HD, NA_WIN_R, NA_WIN_C = 8, 64, 8, 16
D_NA = NA_HEADS * NA_HD
RW_HEADS, RW_HD = 8, 64
D_RW = RW_HEADS * RW_HD
DECAY_LORA, AAA_LORA, GATE_LORA = 64, 64, 160
GN_EPS = 64e-5
MEM_HEADS, MEM_HD = 4, 128
D_MEM = MEM_HEADS * MEM_HD
C_NA = 3 * D_NA
C_RW = 3 * D_RW + 2 * DECAY_LORA + 2 * AAA_LORA + GATE_LORA
C_RW_PAD = 2048
C_GATE = 3 * D_MODEL
N_IN_PAD = C_NA + C_RW_PAD + D_MEM + C_GATE

TOKEN_TILE = 512
FF_CHUNK = 256
NA_ROWS_PER_STEP = 8
NA_ROWS_UNROLL = 4
RW_CHUNK = 64
RW_STEP = 256
RW_TILE_HEADS = 2
RW_GROUP = 4
LANES = 128
VMEM_LIMIT = 56 * 1024 * 1024
NEG_BIG = -1e30


def _const_spec(shape):
    nd = len(shape)
    return pl.BlockSpec(shape, lambda *_: (0,) * nd, pipeline_mode=pl.Buffered(1))


def _dot(a, b):
    return jnp.dot(a, b, preferred_element_type=F32)


def _dot_nt(a, b):
    return lax.dot_general(a, b, (((1,), (1,)), ((), ())), preferred_element_type=F32)


def _dot_tn(a, b):
    return lax.dot_general(a, b, (((0,), (0,)), ((), ())), preferred_element_type=F32)


def _split_dot(x, m, parts):
    acc = None
    rem = x
    for i in range(parts):
        piece = rem.astype(BF16)
        term = _dot(piece, m)
        acc = term if acc is None else acc + term
        if i + 1 < parts:
            rem = rem - piece.astype(F32)
    return acc


def _rmsnorm_rows(x, g):
    ms = jnp.mean(x * x, axis=-1, keepdims=True)
    return x * lax.rsqrt(ms + NORM_EPS) * g


def _ffn_body(x, g_ref, wg_ref, wu_ref, wd_ref, a_scr):
    h = _rmsnorm_rows(x, g_ref[...]).astype(BF16)
    for f in range(0, D_FF, FF_CHUNK):
        gate = _dot(h, wg_ref[:, f:f + FF_CHUNK])
        up = _dot(h, wu_ref[:, f:f + FF_CHUNK])
        a_scr[:, f:f + FF_CHUNK] = (gate * jax.nn.sigmoid(gate) * up).astype(BF16)
    return x + 0.5 * _dot(a_scr[...], wd_ref[...])


def _ffn_kernel(x_ref, g_ref, wg_ref, wu_ref, wd_ref, o_ref, a_scr):
    o_ref[...] = _ffn_body(x_ref[...], g_ref, wg_ref, wu_ref, wd_ref, a_scr)


def _ffn(x, g, wg, wu, wd):
    t = x.shape[0]
    tm = TOKEN_TILE
    return pl.pallas_call(
        _ffn_kernel,
        out_shape=jax.ShapeDtypeStruct((t, D_MODEL), F32),
        grid=(t // tm,),
        in_specs=[
            pl.BlockSpec((tm, D_MODEL), lambda i: (i, 0)),
            _const_spec((1, D_MODEL)),
            _const_spec((D_MODEL, D_FF)),
            _const_spec((D_MODEL, D_FF)),
            _const_spec((D_FF, D_MODEL)),
        ],
        out_specs=pl.BlockSpec((tm, D_MODEL), lambda i: (i, 0)),
        scratch_shapes=[pltpu.VMEM((tm, D_FF), BF16)],
        compiler_params=pltpu.CompilerParams(
            dimension_semantics=("parallel",), vmem_limit_bytes=VMEM_LIMIT),
        name="ffn",
    )(x, g, wg, wu, wd)


def _proj_kernel(x_ref, xp_ref, xn_ref, g_ref, w_ref, gq_ref, gk_ref, gmq_ref, bd_ref, conv_ref,
                 q_ref, k_ref, v_ref, zrw_ref, mq_ref, gate_ref, *, tiles_per_seq):
    tm = TOKEN_TILE
    h_all = _rmsnorm_rows(jnp.concatenate([x_ref[...], xp_ref[...], xn_ref[...]], axis=0),
                          g_ref[...]).astype(BF16)
    h = h_all[:tm]
    bd = bd_ref[...]
    i = pl.program_id(0)
    pos = i % tiles_per_seq
    has_prev = (pos > 0).astype(F32)
    has_next = (pos < tiles_per_seq - 1).astype(F32)
    row8 = lax.broadcasted_iota(jnp.int32, (8, 1), 0)

    def head_norm(z, gain):
        ms = _split_dot(z * z, bd, 2) * (1.0 / NA_HD)
        return z * lax.rsqrt(ms + NORM_EPS) * gain

    zq = _dot(h, w_ref[:, 0:D_NA])
    q_ref[...] = (head_norm(zq, gq_ref[...]) * (NA_HD ** -0.5)).astype(BF16)
    zk = _dot(h, w_ref[:, D_NA:2 * D_NA])
    k_ref[...] = head_norm(zk, gk_ref[...]).astype(BF16)
    v_ref[...] = _dot(h, w_ref[:, 2 * D_NA:3 * D_NA]).astype(BF16)

    chunks = list(range(0, C_RW_PAD, 512))
    z_next = _dot(h_all, w_ref[:, C_NA:C_NA + 512])
    for n, c in enumerate(chunks):
        z_all = z_next
        if n + 1 < len(chunks):
            z_next = _dot(h_all, w_ref[:, C_NA + c + 512:C_NA + c + 1024])
        z = z_all[:tm]
        before = z_all[tm + 7:tm + 8] * has_prev
        after = z_all[tm + 8:tm + 9] * has_next
        down = pltpu.roll(z, 1, 0)
        up = pltpu.roll(z, tm - 1, 0)
        zm1 = jnp.concatenate([jnp.where(row8 == 0, before, down[:8]), down[8:]], axis=0)
        zp1 = jnp.concatenate([up[:tm - 8], jnp.where(row8 == 7, after, up[tm - 8:])], axis=0)
        zrw_ref[:, c:c + 512] = (zm1 * conv_ref[0:1, c:c + 512] + z * conv_ref[1:2, c:c + 512]
                                 + zp1 * conv_ref[2:3, c:c + 512])

    off = C_NA + C_RW_PAD
    zm = _dot(h, w_ref[:, off:off + D_MEM])
    gmq = gmq_ref[...]
    for hd in range(MEM_HEADS):
        sl = slice(hd * MEM_HD, (hd + 1) * MEM_HD)
        mq_ref[:, sl] = (_rmsnorm_rows(zm[:, sl], gmq) * (MEM_HD ** -0.5)).astype(BF16)

    off += D_MEM
    for c in range(0, C_GATE, 512):
        zg = _dot(h, w_ref[:, off + c:off + c + 512])
        gate_ref[:, c:c + 512] = jax.nn.sigmoid(zg).astype(BF16)


def _proj(x, g, w_in_p, gq, gk, gmq, bd64, conv_w, seq):
    t = x.shape[0]
    tm = TOKEN_TILE
    assert seq % tm == 0
    sub = tm // 8
    nblk8 = t // 8
    tok = lambda n: pl.BlockSpec((tm, n), lambda i: (i, 0))
    prev = pl.BlockSpec((8, D_MODEL), lambda i: (jnp.maximum(i * sub - 1, 0), 0))
    nxt = pl.BlockSpec((8, D_MODEL), lambda i: (jnp.minimum((i + 1) * sub, nblk8 - 1), 0))
    return pl.pallas_call(
        functools.partial(_proj_kernel, tiles_per_seq=seq // tm),
        out_shape=(
            jax.ShapeDtypeStruct((t, D_NA), BF16),
            jax.ShapeDtypeStruct((t, D_NA), BF16),
            jax.ShapeDtypeStruct((t, D_NA), BF16),
            jax.ShapeDtypeStruct((t, C_RW_PAD), F32),
            jax.ShapeDtypeStruct((t, D_MEM), BF16),
            jax.ShapeDtypeStruct((t, C_GATE), BF16),
        ),
        grid=(t // tm,),
        in_specs=[
            tok(D_MODEL), prev, nxt,
            _const_spec((1, D_MODEL)),
            _const_spec((D_MODEL, N_IN_PAD)),
            _const_spec((1, D_NA)),
            _const_spec((1, D_NA)),
            _const_spec((1, MEM_HD)),
            _const_spec((D_NA, D_NA)),
            _const_spec((3, C_RW_PAD)),
        ],
        out_specs=(tok(D_NA), tok(D_NA), tok(D_NA), tok(C_RW_PAD), tok(D_MEM), tok(C_GATE)),
        compiler_params=pltpu.CompilerParams(
            dimension_semantics=("parallel",), vmem_limit_bytes=VMEM_LIMIT),
        name="proj",
    )(x, x, x, g, w_in_p, gq, gk, gmq, bd64, conv_w)


def _na_kernel(q_ref, k_ref, v_ref, bias_ref, o_ref, *, rows):
    j = pl.program_id(1)
    lane = lax.broadcasted_iota(jnp.int32, (GRID_W, LANES), 1)
    head0 = lane < NA_HD
    nkeys = NA_WIN_R * GRID_W

    def rows_body(it, carry):
        probs = []
        for u in range(NA_ROWS_UNROLL):
            rr = it * NA_ROWS_UNROLL + u
            r = j * NA_ROWS_PER_STEP + rr
            r0 = jnp.clip(r - NA_WIN_R // 2, 0, rows - NA_WIN_R)
            qoff = pl.multiple_of(rr * GRID_W, GRID_W)
            koff = pl.multiple_of(r0 * GRID_W, GRID_W)
            for p in range(NA_HEADS // 2):
                probs.append((r - r0, qoff, koff, p, slice(p * LANES, (p + 1) * LANES)))
        scores = []
        for d, qoff, koff, p, ls in probs:
            qp = q_ref[pl.ds(qoff, GRID_W), ls]
            zero = jnp.zeros_like(qp)
            qst = jnp.concatenate([jnp.where(head0, qp, zero), jnp.where(head0, zero, qp)], axis=0)
            scores.append(_dot_nt(qst, k_ref[pl.ds(koff, nkeys), ls]) + bias_ref[d, p])
        weights, inv_sums = [], []
        for s in scores:
            e = jnp.exp(s - jnp.max(s, axis=-1, keepdims=True))
            inv_sums.append(1.0 / jnp.sum(e, axis=-1, keepdims=True))
            weights.append(e.astype(BF16))
        outs = [_dot(w, v_ref[pl.ds(koff, nkeys), ls]) * inv
                for w, inv, (d, qoff, koff, p, ls) in zip(weights, inv_sums, probs)]
        for o, (d, qoff, koff, p, ls) in zip(outs, probs):
            o_ref[pl.ds(qoff, GRID_W), ls] = jnp.where(head0, o[:GRID_W], o[GRID_W:]).astype(BF16)
        return carry

    lax.fori_loop(0, NA_ROWS_PER_STEP // NA_ROWS_UNROLL, rows_body, 0)


def _na(q, k, v, bias, batch, seq):
    rows = seq // GRID_W
    assert rows >= NA_WIN_R and rows % NA_ROWS_PER_STEP == 0
    steps = rows // NA_ROWS_PER_STEP
    tq = NA_ROWS_PER_STEP * GRID_W
    return pl.pallas_call(
        functools.partial(_na_kernel, rows=rows),
        out_shape=jax.ShapeDtypeStruct((batch * seq, D_NA), BF16),
        grid=(batch, steps),
        in_specs=[
            pl.BlockSpec((tq, D_NA), lambda b, j: (b * steps + j, 0)),
            pl.BlockSpec((seq, D_NA), lambda b, j: (b, 0)),
            pl.BlockSpec((seq, D_NA), lambda b, j: (b, 0)),
            _const_spec(bias.shape),
        ],
        out_specs=pl.BlockSpec((tq, D_NA), lambda b, j: (b * steps + j, 0)),
        compiler_params=pltpu.CompilerParams(
            dimension_semantics=("parallel", "arbitrary"), vmem_limit_bytes=VMEM_LIMIT),
        name="natten",
    )(q, k, v, bias)


def _na_bias_table(rpb):
    wc = NA_WIN_C
    cols = np.arange(GRID_W)
    col_start = np.clip(cols - wc // 2, 0, GRID_W - wc)
    cc = np.arange(GRID_W)
    valid = (cc[None, :] >= col_start[:, None]) & (cc[None, :] < col_start[:, None] + wc)
    padded = jnp.pad(rpb, ((0, 0), (0, 0), (GRID_W - 1, GRID_W - 1)))
    by_col = jnp.stack([padded[:, :, GRID_W + NA_WIN_C - 2 - c0:2 * GRID_W + NA_WIN_C - 2 - c0]
                        for c0 in range(GRID_W)], axis=2)
    tab = jnp.stack([by_col[:, NA_WIN_R - 1 - d0:2 * NA_WIN_R - 1 - d0] for d0 in range(NA_WIN_R)],
                    axis=1)
    tab = jnp.where(valid[None, None, None], tab, NEG_BIG)
    tab = tab.transpose(1, 0, 3, 2, 4).reshape(NA_WIN_R, NA_HEADS // 2, 2 * GRID_W, NA_WIN_R * GRID_W)
    return tab.astype(F32)


def _memkv_kernel(m_ref, g_ref, w_ref, gk_ref, km_ref, vm_ref):
    h = _rmsnorm_rows(m_ref[...], g_ref[...]).astype(BF16)
    kv = _dot(h, w_ref[...])
    gk = gk_ref[...]
    for hd in range(MEM_HEADS):
        sl = slice(hd * MEM_HD, (hd + 1) * MEM_HD)
        km_ref[:, sl] = _rmsnorm_rows(kv[:, sl], gk).astype(BF16)
    vm_ref[...] = kv[:, D_MEM:].astype(BF16)


def _memkv(mem2d, g, w, gk):
    n = mem2d.shape[0]
    blk = pl.BlockSpec((N_MEM, D_MEM), lambda i: (i, 0))
    return pl.pallas_call(
        _memkv_kernel,
        out_shape=(jax.ShapeDtypeStruct((n, D_MEM), BF16), jax.ShapeDtypeStruct((n, D_MEM), BF16)),
        grid=(n // N_MEM,),
        in_specs=[
            pl.BlockSpec((N_MEM, D_MODEL), lambda i: (i, 0)),
            _const_spec((1, D_MODEL)),
            _const_spec((D_MODEL, 2 * D_MEM)),
            _const_spec((1, MEM_HD)),
        ],
        out_specs=(blk, blk),
        compiler_params=pltpu.CompilerParams(
            dimension_semantics=("parallel",), vmem_limit_bytes=VMEM_LIMIT),
        name="memkv",
    )(mem2d, g, w, gk)


def _memattn_kernel(q_ref, km_ref, vm_ref, o_ref):
    for hd in range(MEM_HEADS):
        sl = slice(hd * MEM_HD, (hd + 1) * MEM_HD)
        s = _dot_nt(q_ref[:, sl], km_ref[:, sl])
        m = jnp.max(s, axis=-1, keepdims=True)
        e = jnp.exp(s - m)
        inv = 1.0 / jnp.sum(e, axis=-1, keepdims=True)
        o_ref[:, sl] = (_dot(e.astype(BF16), vm_ref[:, sl]) * inv).astype(BF16)


def _memattn(q, km, vm, seq):
    t = q.shape[0]
    tm = TOKEN_TILE
    assert seq % tm == 0
    mem_blk = pl.BlockSpec((N_MEM, D_MEM), lambda i: ((i * tm) // seq, 0))
    return pl.pallas_call(
        _memattn_kernel,
        out_shape=jax.ShapeDtypeStruct((t, D_MEM), BF16),
        grid=(t // tm,),
        in_specs=[pl.BlockSpec((tm, D_MEM), lambda i: (i, 0)), mem_blk, mem_blk],
        out_specs=pl.BlockSpec((tm, D_MEM), lambda i: (i, 0)),
        compiler_params=pltpu.CompilerParams(
            dimension_semantics=("parallel",), vmem_limit_bytes=VMEM_LIMIT),
        name="memattn",
    )(q, km, vm)


def _softplus(x):
    return jnp.maximum(x, 0.0) + jnp.log(1.0 + jnp.exp(-jnp.abs(x)))


def _rwkv_kernel(z_ref, w0_ref, w2_ref, a0_ref, a2_ref, g2_ref,
                 kk_ref, ka_ref, rk_ref, lnw_ref, lnb_ref, bd_ref, *rest, rev, nsteps):
    if rev:
        yf_ref, o_ref, *scr = rest
    else:
        o_ref, *scr = rest
    state, rhat_s, yhat_s, m_s, g_s = scr[:5]
    staged = scr[5:13]
    staged_cur = scr[13:15]
    staged_nxt = scr[15:17]
    cn = RW_CHUNK
    ts = RW_STEP
    nck = ts // cn
    j = pl.program_id(0)
    d = 1 if rev else 0
    at_s, rt_s, bt_s, kt_s, bh_s, kh_s, v_s, pc_s = staged
    at_w, rt_w, bt_w, kt_w, bh_w, kh_w, v_w, pc_w = staged
    if rev:
        bonus_s, gate_s = staged_cur
        bonus_w, gate_w = staged_nxt

    @pl.when(j == 0)
    def _():
        for s in list(staged) + list(staged_cur):
            s[...] = jnp.zeros_like(s)

    @pl.when(lax.rem(jnp.maximum(j - 1, 0), nsteps) == 0)
    def _():
        state[...] = jnp.zeros_like(state)

    gr = RW_GROUP * cn
    ngroups = ts // gr
    bd = bd_ref[...]
    shift = cn.bit_length() - 1
    cb = min(gr, 4 * cn)
    ti = lax.broadcasted_iota(jnp.int32, (cb, cb), 0)
    ii = lax.broadcasted_iota(jnp.int32, (cb, cb), 1)
    incl = ((ti >> shift) == (ii >> shift)) & ((ii >= ti) if rev else (ii <= ti))
    incl_b = jnp.where(incl, 1.0, 0.0).astype(BF16)

    def prepare(g):
        r0 = g * gr
        rows = slice(r0, r0 + gr)

        def conv(lo, hi):
            return z_ref[rows, lo:hi]

        r = conv(0, D_RW)
        yield
        k = conv(D_RW, 2 * D_RW)
        yield
        v = conv(2 * D_RW, 3 * D_RW)
        v_w[rows, :] = v.astype(BF16)
        yield
        xw = conv(3 * D_RW, 3 * D_RW + LANES)
        xa = conv(3 * D_RW + LANES, 3 * D_RW + 2 * LANES).astype(BF16)

        def in_context_rate(dd):
            return jax.nn.sigmoid(a0_ref[dd:dd + 1, :] + _dot(xa, a2_ref[dd]))

        wpre = w0_ref[d:d + 1, :] + _dot(jnp.tanh(xw).astype(BF16), w2_ref[d])
        logw = -jnp.exp(-_softplus(-wpre) - 0.5)
        yield
        a = in_context_rate(d)
        kkr = k * kk_ref[...]
        kk = kkr * lax.rsqrt(jnp.maximum(_split_dot(kkr * kkr, bd, 2), 1e-24))
        yield
        kd = k * (1.0 + (a - 1.0) * ka_ref[...])
        beta = kk * a
        yield

        if rev:
            a_fwd = in_context_rate(0)
            kd_fwd = k * (1.0 + (a_fwd - 1.0) * ka_ref[...])
            bonus_w[rows, :] = _split_dot(r * (kd_fwd * rk_ref[0:1, :] + kd * rk_ref[1:2, :]), bd, 2) * v
            yield
            xg = conv(3 * D_RW + 2 * LANES, C_RW_PAD)
            gate_w[rows, :] = _dot(jax.nn.sigmoid(xg).astype(BF16), g2_ref[...])
            yield

        logp = jnp.concatenate([_split_dot_left(incl_b, logw[i:i + cb]) for i in range(0, gr, cb)], axis=0)
        ends =[cl * cn if rev else cl * cn + cn - 1 for cl in range(RW_GROUP)]
        logpc = jnp.concatenate(
            [jnp.broadcast_to(logp[e:e + 1, :], (cn, D_RW)) for e in ends], axis=0)
        for cl, e in enumerate(ends):
            cc = g * RW_GROUP + cl
            pc_w[cc * 8:(cc + 1) * 8, :] = jnp.broadcast_to(jnp.exp(logp[e:e + 1, :]), (8, D_RW))
        yield
        at_w[rows, :] = (-kk * jnp.exp(logp - logw)).astype(BF16)
        rt_w[rows, :] = (r * jnp.exp(logp)).astype(BF16)
        yield
        einv = jnp.exp(-logp)
        bt_w[rows, :] = (beta * einv).astype(BF16)
        kt_w[rows, :] = (kd * einv).astype(BF16)
        yield
        efin = jnp.exp(logpc - logp)
        bh_w[rows, :] = (beta * efin).astype(BF16)
        kh_w[rows, :] = (kd * efin).astype(BF16)
        yield

    nh = RW_TILE_HEADS
    tw = nh * RW_HD
    lane = lax.broadcasted_iota(jnp.int32, (cn, tw), 1)
    lane_head = lane >> shift

    def stack(x):
        zero = jnp.zeros_like(x)
        return jnp.concatenate([jnp.where(lane_head == hh, x, zero) for hh in range(nh)], axis=0)

    ri = lax.broadcasted_iota(jnp.int32, (tw, tw), 0)
    ci = lax.broadcasted_iota(jnp.int32, (tw, tw), 1)
    same_head = (ri >> shift) == (ci >> shift)
    t2 = lax.broadcasted_iota(jnp.int32, (cn, tw), 0)
    i2 = lane & (cn - 1)
    incl2 = (i2 >= t2) if rev else (i2 <= t2)
    strict2 = (i2 > t2) if rev else (i2 < t2)
    eye2 = jnp.where(i2 == t2, 1.0, 0.0)

    npair = RW_HEADS // nh

    def tile(ref, cc, p):
        return ref[cc * cn:(cc + 1) * cn, p * tw:(p + 1) * tw]

    vp, bhp, khp, pcp = {}, {}, {}, {}

    def solve(g):
        g0 = g * RW_GROUP
        probs = [(cc, p) for cc in range(g0, g0 + RW_GROUP) for p in range(npair)]
        ast, vst, rtp, tinv, pw, aak, arb, ark = {}, {}, {}, {}, {}, {}, {}, {}
        for q in probs:
            at_q = tile(at_s, *q)
            ast[q] = stack(at_q)
            vp[q] = tile(v_s, *q)
            vst[q] = stack(vp[q])
            rtp[q] = tile(rt_s, *q)
            bhp[q] = tile(bh_s, *q)
            khp[q] = tile(kh_s, *q)
            pcp[q] = pc_s[q[0] * 8:q[0] * 8 + 1, q[1] * tw:(q[1] + 1) * tw]
            lhs = jnp.concatenate([at_q, rtp[q]], axis=0)
            rhs = jnp.concatenate([stack(tile(bt_s, *q)), stack(tile(kt_s, *q))], axis=0)
            res = _dot_nt(lhs, rhs)
            pw[q] = jnp.where(strict2, res[:cn, :tw], 0.0)
            aak[q] = jnp.where(strict2, res[:cn, tw:], 0.0).astype(BF16)
            arb[q] = jnp.where(incl2, res[cn:, :tw], 0.0).astype(BF16)
            ark[q] = jnp.where(incl2, res[cn:, tw:], 0.0).astype(BF16)
            tinv[q] = eye2 + pw[q]
        yield
        for q in probs:
            pw_b = pw[q].astype(BF16)
            pw[q] = _dot(pw_b, stack(pw_b))
        yield
        for _ in range(4):
            for q in probs:
                pw_b = pw[q].astype(BF16)
                both = _dot(jnp.concatenate([tinv[q].astype(BF16), pw_b], axis=0), stack(pw_b))
                tinv[q] = tinv[q] + both[:cn]
                pw[q] = both[cn:]
            yield
        for q in probs:
            tinv[q] = tinv[q] + _dot(tinv[q].astype(BF16), stack(pw[q].astype(BF16)))
        yield
        wv = {q: _dot(aak[q], vst[q]).astype(BF16) for q in probs}
        yield
        ta = {q: _dot(tinv[q].astype(BF16), jnp.concatenate([ast[q], stack(wv[q])], axis=1)).astype(BF16)
              for q in probs}
        yield
        for q in probs:
            cc, p = q
            ah_q, uh_q = ta[q][:, :tw], ta[q][:, tw:]
            tile_rows = slice(cc * cn, (cc + 1) * cn)
            ls = slice(p * tw, (p + 1) * tw)
            rhs = jnp.concatenate([jnp.concatenate([stack(ah_q), stack(uh_q)], axis=1),
                                   jnp.concatenate([jnp.zeros((tw, tw), BF16), vst[q]], axis=1)], axis=0)
            ry = _dot(jnp.concatenate([arb[q], ark[q]], axis=1), rhs)
            rhat_s[tile_rows, ls] = (rtp[q].astype(F32) + ry[:, :tw]).astype(BF16)
            yhat_s[tile_rows, ls] = ry[:, tw:]
        yield
        for q in probs:
            cc, p = q
            ah_q, uh_q = ta[q][:, :tw], ta[q][:, tw:]
            bh_q = bhp[q]
            m_bd = _dot_tn(ah_q, bh_q)
            m_s[cc * npair + p] = jnp.where(same_head, m_bd, 0.0).astype(BF16)
            g_bd = _dot_tn(jnp.concatenate([uh_q, vp[q]], axis=0),
                           jnp.concatenate([bh_q, khp[q]], axis=0))
            g_bd = jnp.where(same_head, g_bd, 0.0)
            g_q = g_bd[:RW_HD]
            for hh in range(1, nh):
                g_q = g_q + g_bd[hh * RW_HD:(hh + 1) * RW_HD]
            g_s[cc * npair + p] = g_q
        yield

    def recur():
        s_cur = [state[p] for p in range(npair)]
        for cc in (range(nck - 1, -1, -1) if rev else range(nck)):
            tile_rows = slice(cc * cn, (cc + 1) * cn)
            for p in range(npair):
                ls = slice(p * tw, (p + 1) * tw)
                s_b = s_cur[p].astype(BF16)
                y = _dot_nt(rhat_s[tile_rows, ls], stack(s_b)) + yhat_s[tile_rows, ls]
                if rev:
                    yhat_s[tile_rows, ls] = y + yf_ref[tile_rows, ls]
                else:
                    o_ref[tile_rows, ls] = y
                s_cur[p] = (s_cur[p] * pcp[(cc, p)]
                            + _dot(s_b, m_s[cc * npair + p]) + g_s[cc * npair + p])
            yield
        for p in range(npair):
            state[p] = s_cur[p]
        if rev:
            for g in range(ngroups):
                rows = slice(g * gr, (g + 1) * gr)
                ytot = yhat_s[rows, :]
                mu = _split_dot(ytot, bd, 2) * (1.0 / RW_HD)
                yc = ytot - mu
                var = _split_dot(yc * yc, bd, 1) * (1.0 / RW_HD)
                yn = yc * lax.rsqrt(var + GN_EPS) * lnw_ref[...] + lnb_ref[...]
                o_ref[rows, :] = ((yn + bonus_s[rows, :]) * gate_s[rows, :]).astype(BF16)
                yield

    def chain(*gens):
        for gen in gens:
            yield from gen

    def interleave(*gens):
        live = list(gens)
        while live:
            live = [gen for gen in live if next(gen, live) is not live]

    interleave(chain(*[solve(g) for g in range(ngroups)], recur()),
               chain(*[prepare(g) for g in range(ngroups)]))
    for cur, nxt in zip(staged_cur, staged_nxt):
        cur[...] = nxt[...]


def _split_dot_left(m, x):
    hi = x.astype(BF16)
    r1 = x - hi.astype(F32)
    mid = r1.astype(BF16)
    lo = (r1 - mid.astype(F32)).astype(BF16)
    return _dot(m, hi) + _dot(m, mid) + _dot(m, lo)


def _rwkv_pass(zrw, yf, wts, batch, seq, rev):
    ts = RW_STEP
    assert seq % ts == 0 and ts == RW_GROUP * RW_CHUNK and RW_CHUNK == RW_HD
    nch = seq // ts
    nck = ts // RW_CHUNK
    npair = RW_HEADS // RW_TILE_HEADS
    tw = RW_TILE_HEADS * RW_HD
    total = batch * nch

    def block(step):
        if not rev:
            return step
        within = lax.rem(step, nch)
        return step - within + (nch - 1 - within)

    zmain = pl.BlockSpec((ts, C_RW_PAD), lambda j: (block(jnp.minimum(j, total - 1)), 0))
    ymain = pl.BlockSpec((ts, D_RW), lambda j: (block(jnp.maximum(j - 1, 0)), 0))
    in_specs = [zmain] + [_const_spec(w.shape) for w in wts]
    args = [zrw] + list(wts)
    scratch = [pltpu.VMEM((npair, RW_HD, tw), F32),
               pltpu.VMEM((ts, D_RW), BF16),
               pltpu.VMEM((ts, D_RW), F32),
               pltpu.VMEM((nck * npair, tw, tw), BF16),
               pltpu.VMEM((nck * npair, RW_HD, tw), F32)]
    scratch += [pltpu.VMEM((ts, D_RW), BF16) for _ in range(7)]
    scratch += [pltpu.VMEM((nck * 8, D_RW), F32)]
    if rev:
        in_specs.append(ymain)
        args.append(yf)
        scratch += [pltpu.VMEM((ts, D_RW), F32) for _ in range(4)]
    return pl.pallas_call(
        functools.partial(_rwkv_kernel, rev=rev, nsteps=nch),
        out_shape=jax.ShapeDtypeStruct((batch * seq, D_RW), BF16 if rev else F32),
        grid=(total + 1,),
        in_specs=in_specs,
        out_specs=ymain,
        scratch_shapes=scratch,
        compiler_params=pltpu.CompilerParams(
            dimension_semantics=("arbitrary",), vmem_limit_bytes=VMEM_LIMIT),
        name="rwkv_bwd" if rev else "rwkv_fwd",
    )(*args)


def _merge_ffn_kernel(x_ref, na_ref, rw_ref, mem_ref, gate_ref, wna_ref, wrw_ref, wmem_ref, wout_ref,
                      g_ref, wg_ref, wu_ref, wd_ref, o_ref, a_scr):
    d = D_MODEL
    merged = (gate_ref[:, 0:d].astype(F32) * _dot(na_ref[...], wna_ref[...])
              + gate_ref[:, d:2 * d].astype(F32) * _dot(rw_ref[...], wrw_ref[...])
              + gate_ref[:, 2 * d:3 * d].astype(F32) * _dot(mem_ref[...], wmem_ref[...]))
    x2 = x_ref[...] + _dot(merged.astype(BF16), wout_ref[...])
    o_ref[...] = _ffn_body(x2, g_ref, wg_ref, wu_ref, wd_ref, a_scr)


def _merge_ffn(x, y_na, y_rw, y_mem, gates, wna, wrw, wmem, wout, g, wg, wu, wd):
    t = x.shape[0]
    tm = TOKEN_TILE
    tok = lambda n: pl.BlockSpec((tm, n), lambda i: (i, 0))
    return pl.pallas_call(
        _merge_ffn_kernel,
        out_shape=jax.ShapeDtypeStruct((t, D_MODEL), F32),
        grid=(t // tm,),
        in_specs=[tok(D_MODEL), tok(D_NA), tok(D_RW), tok(D_MEM), tok(C_GATE),
                  _const_spec((D_NA, D_MODEL)), _const_spec((D_RW, D_MODEL)),
                  _const_spec((D_MEM, D_MODEL)), _const_spec((D_MODEL, D_MODEL)),
                  _const_spec((1, D_MODEL)), _const_spec((D_MODEL, D_FF)),
                  _const_spec((D_MODEL, D_FF)), _const_spec((D_FF, D_MODEL))],
        out_specs=tok(D_MODEL),
        scratch_shapes=[pltpu.VMEM((tm, D_FF), BF16)],
        compiler_params=pltpu.CompilerParams(
            dimension_semantics=("parallel",), vmem_limit_bytes=VMEM_LIMIT),
        name="merge_ffn",
    )(x, y_na, y_rw, y_mem, gates, wna, wrw, wmem, wout, g, wg, wu, wd)


def _block_diag_ones(n, blk):
    idx = np.arange(n) // blk
    return jnp.asarray(idx[:, None] == idx[None, :], dtype=BF16)


def _prep_params(p):
    row = lambda a: a.reshape(1, -1).astype(F32)
    w_in = p['w_in']
    rw_lo, rw_hi = C_NA, C_NA + C_RW
    w_in_p = jnp.concatenate([
        w_in[:, :C_NA],
        jnp.pad(w_in[:, rw_lo:rw_hi], ((0, 0), (0, C_RW_PAD - C_RW))),
        w_in[:, rw_hi:],
    ], axis=1).astype(BF16)

    def lora_stack(w):
        z = jnp.zeros_like(w[0])
        return jnp.stack([jnp.concatenate([w[0], z], axis=0),
                          jnp.concatenate([z, w[1]], axis=0)]).astype(BF16)

    q = {
        'g_ffn1': row(p['g_ffn1']), 'g_ffn2': row(p['g_ffn2']), 'g_mix': row(p['g_mix']),
        'wg1': p['w_ffn1_gate'].astype(BF16), 'wu1': p['w_ffn1_up'].astype(BF16),
        'wd1': p['w_ffn1_down'].astype(BF16),
        'wg2': p['w_ffn2_gate'].astype(BF16), 'wu2': p['w_ffn2_up'].astype(BF16),
        'wd2': p['w_ffn2_down'].astype(BF16),
        'w_in_p': w_in_p,
        'gq': row(jnp.tile(p['g_qn_na'], NA_HEADS)), 'gk': row(jnp.tile(p['g_kn_na'], NA_HEADS)),
        'gmq': row(p['g_qn_mem']), 'gmk': row(p['g_kn_mem']),
        'bd64': _block_diag_ones(D_NA, NA_HD),
        'na_bias': _na_bias_table(p['rpb_na'].astype(F32)),
        'g_mem_norm': row(p['g_mem_norm']), 'w_mem_kv': p['w_mem_kv'].astype(BF16),
        'w_o_na': p['w_o_na'].astype(BF16), 'w_o_rw': p['w_o_rw'].astype(BF16),
        'w_o_mem': p['w_o_mem'].astype(BF16), 'w_out': p['w_out'].astype(BF16),
    }
    q['conv'] = jnp.pad(p['conv_rw'].astype(F32), ((0, 0), (0, C_RW_PAD - C_RW)))
    q['rw'] = (
        p['w0_rw'].astype(F32), lora_stack(p['w2_rw']),
        p['a0_rw'].astype(F32), lora_stack(p['a2_rw']),
        jnp.pad(p['g2_rw'], ((0, 2 * LANES - GATE_LORA), (0, 0))).astype(BF16),
        row(p['k_k_rw']), row(p['k_a_rw']), p['r_k_rw'].reshape(2, D_RW).astype(F32),
        row(p['ln_x_w_rw']), row(p['ln_x_b_rw']),
        q['bd64'],
    )
    return q


def _encoder_layer(x, mem, q):
    batch, seq, _ = x.shape
    x2d = x.reshape(batch * seq, D_MODEL)
    x1 = _ffn(x2d, q['g_ffn1'], q['wg1'], q['wu1'], q['wd1'])
    qn, kn, vv, zrw, mq, gates = _proj(x1, q['g_mix'], q['w_in_p'], q['gq'], q['gk'], q['gmq'], q['bd64'],
                                       q['conv'], seq)
    y_na = _na(qn, kn, vv, q['na_bias'], batch, seq)
    km, vm = _memkv(mem.reshape(batch * N_MEM, D_MODEL), q['g_mem_norm'], q['w_mem_kv'], q['gmk'])
    y_mem = _memattn(mq, km, vm, seq)
    y_fwd = _rwkv_pass(zrw, None, q['rw'], batch, seq, rev=False)
    y_rw = _rwkv_pass(zrw, y_fwd, q['rw'], batch, seq, rev=True)
    x3 = _merge_ffn(x1, y_na, y_rw, y_mem, gates, q['w_o_na'], q['w_o_rw'], q['w_o_mem'], q['w_out'],
                    q['g_ffn2'], q['wg2'], q['wu2'], q['wd2'])
    return x3.reshape(batch, seq, D_MODEL)


def kernel(x_prompt, x_sample, mem_prompt, mem_sample, g_ffn1, w_ffn1_gate, w_ffn1_up, w_ffn1_down, g_mix, w_in, g_qn_na, g_kn_na, rpb_na, w_o_na, conv_rw, w0_rw, w2_rw, a0_rw, a2_rw, g2_rw, k_k_rw, k_a_rw, r_k_rw, ln_x_w_rw, ln_x_b_rw, w_o_rw, g_mem_norm, w_mem_kv, g_qn_mem, g_kn_mem, w_o_mem, w_out, g_ffn2, w_ffn2_gate, w_ffn2_up, w_ffn2_down):
    params = {
        'g_ffn1': g_ffn1, 'w_ffn1_gate': w_ffn1_gate, 'w_ffn1_up': w_ffn1_up,
        'w_ffn1_down': w_ffn1_down, 'g_mix': g_mix, 'w_in': w_in,
        'g_qn_na': g_qn_na, 'g_kn_na': g_kn_na, 'rpb_na': rpb_na, 'w_o_na': w_o_na,
        'conv_rw': conv_rw, 'w0_rw': w0_rw, 'w2_rw': w2_rw, 'a0_rw': a0_rw, 'a2_rw': a2_rw,
        'g2_rw': g2_rw, 'k_k_rw': k_k_rw, 'k_a_rw': k_a_rw, 'r_k_rw': r_k_rw,
        'ln_x_w_rw': ln_x_w_rw, 'ln_x_b_rw': ln_x_b_rw, 'w_o_rw': w_o_rw,
        'g_mem_norm': g_mem_norm, 'w_mem_kv': w_mem_kv, 'g_qn_mem': g_qn_mem,
        'g_kn_mem': g_kn_mem, 'w_o_mem': w_o_mem, 'w_out': w_out,
        'g_ffn2': g_ffn2, 'w_ffn2_gate': w_ffn2_gate, 'w_ffn2_up': w_ffn2_up,
        'w_ffn2_down': w_ffn2_down,
    }
    depth = g_ffn1.shape[0]
    y_prompt, y_sample = x_prompt, x_sample
    for layer in range(depth):
        q = _prep_params({name: arr[layer] for name, arr in params.items()})
        y_prompt = _encoder_layer(y_prompt, mem_prompt, q)
        y_sample = _encoder_layer(y_sample, mem_sample, q)
    return (y_prompt, y_sample)
```

```python
import functools

import jax
import jax.numpy as jnp
import numpy as np
from jax import lax
from jax.experimental import pallas as pl
from jax.experimental.pallas import tpu as pltpu

F32 = jnp.float32
BF16 = jnp.bfloat16

D_MODEL = 1024
D_FF = 2816
NORM_EPS = 1e-6
GRID_W = 64
N_MEM = 256
NA_HEADS, NA_HD, NA_WIN_R, NA_WIN_C = 8, 64, 8, 16
D_NA = NA_HEADS * NA_HD
RW_HEADS, RW_HD = 8, 64
D_RW = RW_HEADS * RW_HD
DECAY_LORA, AAA_LORA, GATE_LORA = 64, 64, 160
GN_EPS = 64e-5
MEM_HEADS, MEM_HD = 4, 128
D_MEM = MEM_HEADS * MEM_HD
C_NA = 3 * D_NA
C_RW = 3 * D_RW + 2 * DECAY_LORA + 2 * AAA_LORA + GATE_LORA
C_RW_PAD = 2048
C_GATE = 3 * D_MODEL
N_IN_PAD = C_NA + C_RW_PAD + D_MEM + C_GATE

TOKEN_TILE = 512
PROJ_TILE = 512
FF_CHUNK = 256
NA_ROWS_PER_STEP = 8
NA_ROWS_UNROLL = 4
RW_CHUNK = 64
RW_STEP = 256
RW_TILE_HEADS = 2
RW_GROUP = 4
LANES = 128
VMEM_LIMIT = 56 * 1024 * 1024
NEG_BIG = -1e30


def _const_spec(shape):
    nd = len(shape)
    return pl.BlockSpec(shape, lambda *_: (0,) * nd, pipeline_mode=pl.Buffered(1))


def _dot(a, b):
    return jnp.dot(a, b, preferred_element_type=F32)


def _dot_nt(a, b):
    return lax.dot_general(a, b, (((1,), (1,)), ((), ())), preferred_element_type=F32)


def _dot_tn(a, b):
    return lax.dot_general(a, b, (((0,), (0,)), ((), ())), preferred_element_type=F32)


def _split_dot(x, m, parts):
    acc = None
    rem = x
    for i in range(parts):
        piece = rem.astype(BF16)
        term = _dot(piece, m)
        acc = term if acc is None else acc + term
        if i + 1 < parts:
            rem = rem - piece.astype(F32)
    return acc


def _rmsnorm_rows(x, g):
    ms = jnp.mean(x * x, axis=-1, keepdims=True)
    return x * lax.rsqrt(ms + NORM_EPS) * g


def _ffn_body(x, g_ref, wg_ref, wu_ref, wd_ref, a_scr):
    h = _rmsnorm_rows(x, g_ref[...]).astype(BF16)
    for f in range(0, D_FF, FF_CHUNK):
        gate = _dot(h, wg_ref[:, f:f + FF_CHUNK])
        up = _dot(h, wu_ref[:, f:f + FF_CHUNK])
        a_scr[:, f:f + FF_CHUNK] = (gate * jax.nn.sigmoid(gate) * up).astype(BF16)
    return x + 0.5 * _dot(a_scr[...], wd_ref[...])


def _ffn_kernel(x_ref, g_ref, wg_ref, wu_ref, wd_ref, o_ref, a_scr):
    o_ref[...] = _ffn_body(x_ref[...], g_ref, wg_ref, wu_ref, wd_ref, a_scr)


def _ffn(x, g, wg, wu, wd):
    t = x.shape[0]
    tm = TOKEN_TILE
    return pl.pallas_call(
        _ffn_kernel,
        out_shape=jax.ShapeDtypeStruct((t, D_MODEL), F32),
        grid=(t // tm,),
        in_specs=[
            pl.BlockSpec((tm, D_MODEL), lambda i: (i, 0)),
            _const_spec((1, D_MODEL)),
            _const_spec((D_MODEL, D_FF)),
            _const_spec((D_MODEL, D_FF)),
            _const_spec((D_FF, D_MODEL)),
        ],
        out_specs=pl.BlockSpec((tm, D_MODEL), lambda i: (i, 0)),
        scratch_shapes=[pltpu.VMEM((tm, D_FF), BF16)],
        compiler_params=pltpu.CompilerParams(
            dimension_semantics=("parallel",), vmem_limit_bytes=VMEM_LIMIT),
        name="ffn",
    )(x, g, wg, wu, wd)


def _proj_kernel(x_ref, xp_ref, xn_ref, km_ref, vm_ref, g_ref, w_ref, gq_ref, gk_ref, gymem_ref, bd_ref,
                 conv_ref, q_ref, k_ref, v_ref, zrw_ref, ymem_ref, gate_ref, *, tiles_per_seq):
    tm = PROJ_TILE
    h_all = _rmsnorm_rows(jnp.concatenate([x_ref[...], xp_ref[...], xn_ref[...]], axis=0),
                          g_ref[...]).astype(BF16)
    h = h_all[:tm]
    bd = bd_ref[...]
    i = pl.program_id(0)
    pos = i % tiles_per_seq
    has_prev = (pos > 0).astype(F32)
    has_next = (pos < tiles_per_seq - 1).astype(F32)
    row8 = lax.broadcasted_iota(jnp.int32, (8, 1), 0)

    def head_norm(z, gain):
        ms = _split_dot(z * z, bd, 1) * (1.0 / NA_HD)
        return z * lax.rsqrt(ms + NORM_EPS) * gain

    zq = _dot(h, w_ref[:, 0:D_NA])
    q_ref[...] = (head_norm(zq, gq_ref[...]) * (NA_HD ** -0.5)).astype(BF16)
    zk = _dot(h, w_ref[:, D_NA:2 * D_NA])
    k_ref[...] = head_norm(zk, gk_ref[...]).astype(BF16)
    v_ref[...] = _dot(h, w_ref[:, 2 * D_NA:3 * D_NA]).astype(BF16)

    chunks = list(range(0, C_RW_PAD, 512))
    z_next = _dot(h_all, w_ref[:, C_NA:C_NA + 512])
    for n, c in enumerate(chunks):
        z_all = z_next
        if n + 1 < len(chunks):
            z_next = _dot(h_all, w_ref[:, C_NA + c + 512:C_NA + c + 1024])
        z = z_all[:tm]
        before = z_all[tm + 7:tm + 8] * has_prev
        after = z_all[tm + 8:tm + 9] * has_next
        down = pltpu.roll(z, 1, 0)
        up = pltpu.roll(z, tm - 1, 0)
        zm1 = jnp.concatenate([jnp.where(row8 == 0, before, down[:8]), down[8:]], axis=0)
        zp1 = jnp.concatenate([up[:tm - 8], jnp.where(row8 == 7, after, up[tm - 8:])], axis=0)
        zrw_ref[:, c:c + 512] = (zm1 * conv_ref[0:1, c:c + 512] + z * conv_ref[1:2, c:c + 512]
                                 + zp1 * conv_ref[2:3, c:c + 512])

    off = C_NA + C_RW_PAD
    zm = _dot(h, w_ref[:, off:off + D_MEM])
    gmq = gymem_ref[...]
    qms = [(_rmsnorm_rows(zm[:, hd * MEM_HD:(hd + 1) * MEM_HD], gmq) * (MEM_HD ** -0.5)).astype(BF16)
           for hd in range(MEM_HEADS)]
    scores = [_dot_nt(qms[hd], km_ref[:, hd * MEM_HD:(hd + 1) * MEM_HD]) for hd in range(MEM_HEADS)]
    for hd in range(MEM_HEADS):
        sl = slice(hd * MEM_HD, (hd + 1) * MEM_HD)
        e = jnp.exp(scores[hd] - jnp.max(scores[hd], axis=-1, keepdims=True))
        inv = 1.0 / jnp.sum(e, axis=-1, keepdims=True)
        ymem_ref[:, sl] = (_dot(e.astype(BF16), vm_ref[:, sl]) * inv).astype(BF16)

    off += D_MEM
    for c in range(0, C_GATE, 512):
        zg = _dot(h, w_ref[:, off + c:off + c + 512])
        gate_ref[:, c:c + 512] = jax.nn.sigmoid(zg).astype(BF16)


def _proj(x, km, vm, g, w_in_p, gq, gk, gmq, bd64, conv_w, seq):
    t = x.shape[0]
    tm = PROJ_TILE
    assert seq % tm == 0
    sub = tm // 8
    nblk8 = t // 8
    tok = lambda n: pl.BlockSpec((tm, n), lambda i: (i, 0))
    prev = pl.BlockSpec((8, D_MODEL), lambda i: (jnp.maximum(i * sub - 1, 0), 0))
    nxt = pl.BlockSpec((8, D_MODEL), lambda i: (jnp.minimum((i + 1) * sub, nblk8 - 1), 0))
    mem_blk = pl.BlockSpec((N_MEM, D_MEM), lambda i: ((i * tm) // seq, 0))
    return pl.pallas_call(
        functools.partial(_proj_kernel, tiles_per_seq=seq // tm),
        out_shape=(
            jax.ShapeDtypeStruct((t, D_NA), BF16),
            jax.ShapeDtypeStruct((t, D_NA), BF16),
            jax.ShapeDtypeStruct((t, D_NA), BF16),
            jax.ShapeDtypeStruct((t, C_RW_PAD), F32),
            jax.ShapeDtypeStruct((t, D_MEM), BF16),
            jax.ShapeDtypeStruct((t, C_GATE), BF16),
        ),
        grid=(t // tm,),
        in_specs=[
            tok(D_MODEL), prev, nxt, mem_blk, mem_blk,
            _const_spec((1, D_MODEL)),
            _const_spec((D_MODEL, N_IN_PAD)),
            _const_spec((1, D_NA)),
            _const_spec((1, D_NA)),
            _const_spec((1, MEM_HD)),
            _const_spec((D_NA, D_NA)),
            _const_spec((3, C_RW_PAD)),
        ],
        out_specs=(tok(D_NA), tok(D_NA), tok(D_NA), tok(C_RW_PAD), tok(D_MEM), tok(C_GATE)),
        compiler_params=pltpu.CompilerParams(
            dimension_semantics=("parallel",), vmem_limit_bytes=VMEM_LIMIT),
        name="proj",
    )(x, x, x, km, vm, g, w_in_p, gq, gk, gmq, bd64, conv_w)


def _na_kernel(q_ref, k_ref, v_ref, bias_ref, o_ref, *, rows):
    j = pl.program_id(1)
    lane = lax.broadcasted_iota(jnp.int32, (GRID_W, LANES), 1)
    head0 = lane < NA_HD
    nkeys = NA_WIN_R * GRID_W

    def rows_body(it, carry):
        probs = []
        for u in range(NA_ROWS_UNROLL):
            rr = it * NA_ROWS_UNROLL + u
            r = j * NA_ROWS_PER_STEP + rr
            r0 = jnp.clip(r - NA_WIN_R // 2, 0, rows - NA_WIN_R)
            qoff = pl.multiple_of(rr * GRID_W, GRID_W)
            koff = pl.multiple_of(r0 * GRID_W, GRID_W)
            for p in range(NA_HEADS // 2):
                probs.append((r - r0, qoff, koff, p, slice(p * LANES, (p + 1) * LANES)))
        scores = []
        for d, qoff, koff, p, ls in probs:
            qp = q_ref[pl.ds(qoff, GRID_W), ls]
            zero = jnp.zeros_like(qp)
            qst = jnp.concatenate([jnp.where(head0, qp, zero), jnp.where(head0, zero, qp)], axis=0)
            scores.append(_dot_nt(qst, k_ref[pl.ds(koff, nkeys), ls]) + bias_ref[d, p])
        weights, inv_sums = [], []
        for s in scores:
            e = jnp.exp(s - jnp.max(s, axis=-1, keepdims=True))
            inv_sums.append(1.0 / jnp.sum(e, axis=-1, keepdims=True))
            weights.append(e.astype(BF16))
        outs = [_dot(w, v_ref[pl.ds(koff, nkeys), ls]) * inv
                for w, inv, (d, qoff, koff, p, ls) in zip(weights, inv_sums, probs)]
        for o, (d, qoff, koff, p, ls) in zip(outs, probs):
            o_ref[pl.ds(qoff, GRID_W), ls] = jnp.where(head0, o[:GRID_W], o[GRID_W:]).astype(BF16)
        return carry

    lax.fori_loop(0, NA_ROWS_PER_STEP // NA_ROWS_UNROLL, rows_body, 0)


def _na(q, k, v, bias, batch, seq):
    rows = seq // GRID_W
    assert rows >= NA_WIN_R and rows % NA_ROWS_PER_STEP == 0
    steps = rows // NA_ROWS_PER_STEP
    tq = NA_ROWS_PER_STEP * GRID_W
    return pl.pallas_call(
        functools.partial(_na_kernel, rows=rows),
        out_shape=jax.ShapeDtypeStruct((batch * seq, D_NA), BF16),
        grid=(batch, steps),
        in_specs=[
            pl.BlockSpec((tq, D_NA), lambda b, j: (b * steps + j, 0)),
            pl.BlockSpec((seq, D_NA), lambda b, j: (b, 0)),
            pl.BlockSpec((seq, D_NA), lambda b, j: (b, 0)),
            _const_spec(bias.shape),
        ],
        out_specs=pl.BlockSpec((tq, D_NA), lambda b, j: (b * steps + j, 0)),
        compiler_params=pltpu.CompilerParams(
            dimension_semantics=("parallel", "arbitrary"), vmem_limit_bytes=VMEM_LIMIT),
        name="natten",
    )(q, k, v, bias)


def _na_bias_table(rpb):
    wc = NA_WIN_C
    cols = np.arange(GRID_W)
    col_start = np.clip(cols - wc // 2, 0, GRID_W - wc)
    cc = np.arange(GRID_W)
    valid = (cc[None, :] >= col_start[:, None]) & (cc[None, :] < col_start[:, None] + wc)
    padded = jnp.pad(rpb, ((0, 0), (0, 0), (GRID_W - 1, GRID_W - 1)))
    by_col = jnp.stack([padded[:, :, GRID_W + NA_WIN_C - 2 - c0:2 * GRID_W + NA_WIN_C - 2 - c0]
                        for c0 in range(GRID_W)], axis=2)
    tab = jnp.stack([by_col[:, NA_WIN_R - 1 - d0:2 * NA_WIN_R - 1 - d0] for d0 in range(NA_WIN_R)],
                    axis=1)
    tab = jnp.where(valid[None, None, None], tab, NEG_BIG)
    tab = tab.transpose(1, 0, 3, 2, 4).reshape(NA_WIN_R, NA_HEADS // 2, 2 * GRID_W, NA_WIN_R * GRID_W)
    return tab.astype(F32)


def _memkv_kernel(m_ref, g_ref, w_ref, gk_ref, km_ref, vm_ref):
    h = _rmsnorm_rows(m_ref[...], g_ref[...]).astype(BF16)
    kv = _dot(h, w_ref[...])
    gk = gk_ref[...]
    for hd in range(MEM_HEADS):
        sl = slice(hd * MEM_HD, (hd + 1) * MEM_HD)
        km_ref[:, sl] = _rmsnorm_rows(kv[:, sl], gk).astype(BF16)
    vm_ref[...] = kv[:, D_MEM:].astype(BF16)


def _memkv(mem2d, g, w, gk):
    n = mem2d.shape[0]
    blk = pl.BlockSpec((N_MEM, D_MEM), lambda i: (i, 0))
    return pl.pallas_call(
        _memkv_kernel,
        out_shape=(jax.ShapeDtypeStruct((n, D_MEM), BF16), jax.ShapeDtypeStruct((n, D_MEM), BF16)),
        grid=(n // N_MEM,),
        in_specs=[
            pl.BlockSpec((N_MEM, D_MODEL), lambda i: (i, 0)),
            _const_spec((1, D_MODEL)),
            _const_spec((D_MODEL, 2 * D_MEM)),
            _const_spec((1, MEM_HD)),
        ],
        out_specs=(blk, blk),
        compiler_params=pltpu.CompilerParams(
            dimension_semantics=("parallel",), vmem_limit_bytes=VMEM_LIMIT),
        name="memkv",
    )(mem2d, g, w, gk)


def _softplus(x):
    return jnp.maximum(x, 0.0) + jnp.log(1.0 + jnp.exp(-jnp.abs(x)))


def _rwkv_kernel(z_ref, w0_ref, w2_ref, a0_ref, a2_ref, g2_ref,
                 kk_ref, ka_ref, rk_ref, lnw_ref, lnb_ref, bd_ref, *rest, rev, nsteps):
    if rev:
        yf_ref, o_ref, *scr = rest
    else:
        o_ref, *scr = rest
    state, rhat_s, yhat_s, m_s, g_s = scr[:5]
    staged = scr[5:13]
    staged_cur = scr[13:15]
    staged_nxt = scr[15:17]
    cn = RW_CHUNK
    ts = RW_STEP
    nck = ts // cn
    j = pl.program_id(0)
    d = 1 if rev else 0
    at_s, rt_s, bt_s, kt_s, bh_s, kh_s, v_s, pc_s = staged
    at_w, rt_w, bt_w, kt_w, bh_w, kh_w, v_w, pc_w = staged
    if rev:
        bonus_s, gate_s = staged_cur
        bonus_w, gate_w = staged_nxt

    @pl.when(j == 0)
    def _():
        for s in list(staged) + list(staged_cur):
            s[...] = jnp.zeros_like(s)

    @pl.when(lax.rem(jnp.maximum(j - 1, 0), nsteps) == 0)
    def _():
        state[...] = jnp.zeros_like(state)

    gr = RW_GROUP * cn
    ngroups = ts // gr
    bd = bd_ref[...]
    shift = cn.bit_length() - 1
    cb = min(gr, 4 * cn)
    ti = lax.broadcasted_iota(jnp.int32, (cb, cb), 0)
    ii = lax.broadcasted_iota(jnp.int32, (cb, cb), 1)
    incl = ((ti >> shift) == (ii >> shift)) & ((ii >= ti) if rev else (ii <= ti))
    incl_b = jnp.where(incl, 1.0, 0.0).astype(BF16)

    def prepare(g):
        r0 = g * gr
        rows = slice(r0, r0 + gr)

        def conv(lo, hi):
            return z_ref[rows, lo:hi]

        r = conv(0, D_RW)
        yield
        k = conv(D_RW, 2 * D_RW)
        yield
        v = conv(2 * D_RW, 3 * D_RW)
        v_w[rows, :] = v.astype(BF16)
        yield
        xw = conv(3 * D_RW, 3 * D_RW + LANES)
        xa = conv(3 * D_RW + LANES, 3 * D_RW + 2 * LANES).astype(BF16)

        def in_context_rate(dd):
            return jax.nn.sigmoid(a0_ref[dd:dd + 1, :] + _dot(xa, a2_ref[dd]))

        wpre = w0_ref[d:d + 1, :] + _dot(jnp.tanh(xw).astype(BF16), w2_ref[d])
        logw = -jnp.exp(-_softplus(-wpre) - 0.5)
        yield
        a = in_context_rate(d)
        kkr = k * kk_ref[...]
        kk = kkr * lax.rsqrt(jnp.maximum(_split_dot(kkr * kkr, bd, 1), 1e-24))
        yield
        kd = k * (1.0 + (a - 1.0) * ka_ref[...])
        beta = kk * a
        yield

        if rev:
            a_fwd = in_context_rate(0)
            kd_fwd = k * (1.0 + (a_fwd - 1.0) * ka_ref[...])
            bonus_w[rows, :] = _split_dot(r * (kd_fwd * rk_ref[0:1, :] + kd * rk_ref[1:2, :]), bd, 2) * v
            yield
            xg = conv(3 * D_RW + 2 * LANES, C_RW_PAD)
            gate_w[rows, :] = _dot(jax.nn.sigmoid(xg).astype(BF16), g2_ref[...])
            yield

        logp = jnp.concatenate([_split_dot_left(incl_b, logw[i:i + cb]) for i in range(0, gr, cb)], axis=0)
        ends =[cl * cn if rev else cl * cn + cn - 1 for cl in range(RW_GROUP)]
        logpc = jnp.concatenate(
            [jnp.broadcast_to(logp[e:e + 1, :], (cn, D_RW)) for e in ends], axis=0)
        for cl, e in enumerate(ends):
            cc = g * RW_GROUP + cl
            pc_w[cc * 8:(cc + 1) * 8, :] = jnp.broadcast_to(jnp.exp(logp[e:e + 1, :]), (8, D_RW))
        yield
        at_w[rows, :] = (-kk * jnp.exp(logp - logw)).astype(BF16)
        rt_w[rows, :] = (r * jnp.exp(logp)).astype(BF16)
        yield
        einv = jnp.exp(-logp)
        bt_w[rows, :] = (beta * einv).astype(BF16)
        kt_w[rows, :] = (kd * einv).astype(BF16)
        yield
        efin = jnp.exp(logpc - logp)
        bh_w[rows, :] = (beta * efin).astype(BF16)
        kh_w[rows, :] = (kd * efin).astype(BF16)
        yield

    nh = RW_TILE_HEADS
    tw = nh * RW_HD
    lane = lax.broadcasted_iota(jnp.int32, (cn, tw), 1)
    lane_head = lane >> shift

    def stack(x):
        zero = jnp.zeros_like(x)
        return jnp.concatenate([jnp.where(lane_head == hh, x, zero) for hh in range(nh)], axis=0)

    ri = lax.broadcasted_iota(jnp.int32, (tw, tw), 0)
    ci = lax.broadcasted_iota(jnp.int32, (tw, tw), 1)
    same_head = (ri >> shift) == (ci >> shift)
    t2 = lax.broadcasted_iota(jnp.int32, (cn, tw), 0)
    i2 = lane & (cn - 1)
    incl2 = (i2 >= t2) if rev else (i2 <= t2)
    strict2 = (i2 > t2) if rev else (i2 < t2)
    eye2 = jnp.where(i2 == t2, 1.0, 0.0)

    npair = RW_HEADS // nh

    def tile(ref, cc, p):
        return ref[cc * cn:(cc + 1) * cn, p * tw:(p + 1) * tw]

    vp, bhp, khp, pcp = {}, {}, {}, {}

    def solve(g):
        g0 = g * RW_GROUP
        probs = [(cc, p) for cc in range(g0, g0 + RW_GROUP) for p in range(npair)]
        ast, vst, rtp, tinv, pw, aak, arb, ark = {}, {}, {}, {}, {}, {}, {}, {}
        for q in probs:
            at_q = tile(at_s, *q)
            ast[q] = stack(at_q)
            vp[q] = tile(v_s, *q)
            vst[q] = stack(vp[q])
            rtp[q] = tile(rt_s, *q)
            bhp[q] = tile(bh_s, *q)
            khp[q] = tile(kh_s, *q)
            pcp[q] = pc_s[q[0] * 8:q[0] * 8 + 1, q[1] * tw:(q[1] + 1) * tw]
            lhs = jnp.concatenate([at_q, rtp[q]], axis=0)
            rhs = jnp.concatenate([stack(tile(bt_s, *q)), stack(tile(kt_s, *q))], axis=0)
            res = _dot_nt(lhs, rhs)
            pw[q] = jnp.where(strict2, res[:cn, :tw], 0.0)
            aak[q] = jnp.where(strict2, res[:cn, tw:], 0.0).astype(BF16)
            arb[q] = jnp.where(incl2, res[cn:, :tw], 0.0).astype(BF16)
            ark[q] = jnp.where(incl2, res[cn:, tw:], 0.0).astype(BF16)
            tinv[q] = eye2 + pw[q]
        yield
        for q in probs:
            pw_b = pw[q].astype(BF16)
            pw[q] = _dot(pw_b, stack(pw_b))
        yield
        for _ in range(4):
            for q in probs:
                pw_b = pw[q].astype(BF16)
                both = _dot(jnp.concatenate([tinv[q].astype(BF16), pw_b], axis=0), stack(pw_b))
                tinv[q] = tinv[q] + both[:cn]
                pw[q] = both[cn:]
            yield
        for q in probs:
            tinv[q] = tinv[q] + _dot(tinv[q].astype(BF16), stack(pw[q].astype(BF16)))
        yield
        wv = {q: _dot(aak[q], vst[q]).astype(BF16) for q in probs}
        yield
        ta = {q: _dot(tinv[q].astype(BF16), jnp.concatenate([ast[q], stack(wv[q])], axis=1)).astype(BF16)
              for q in probs}
        yield
        for q in probs:
            cc, p = q
            ah_q, uh_q = ta[q][:, :tw], ta[q][:, tw:]
            tile_rows = slice(cc * cn, (cc + 1) * cn)
            ls = slice(p * tw, (p + 1) * tw)
            rhs = jnp.concatenate([jnp.concatenate([stack(ah_q), stack(uh_q)], axis=1),
                                   jnp.concatenate([jnp.zeros((tw, tw), BF16), vst[q]], axis=1)], axis=0)
            ry = _dot(jnp.concatenate([arb[q], ark[q]], axis=1), rhs)
            rhat_s[tile_rows, ls] = (rtp[q].astype(F32) + ry[:, :tw]).astype(BF16)
            yhat_s[tile_rows, ls] = ry[:, tw:]
        yield
        for q in probs:
            cc, p = q
            ah_q, uh_q = ta[q][:, :tw], ta[q][:, tw:]
            bh_q = bhp[q]
            m_bd = _dot_tn(ah_q, bh_q)
            m_s[cc * npair + p] = jnp.where(same_head, m_bd, 0.0).astype(BF16)
            g_bd = _dot_tn(jnp.concatenate([uh_q, vp[q]], axis=0),
                           jnp.concatenate([bh_q, khp[q]], axis=0))
            g_bd = jnp.where(same_head, g_bd, 0.0)
            g_q = g_bd[:RW_HD]
            for hh in range(1, nh):
                g_q = g_q + g_bd[hh * RW_HD:(hh + 1) * RW_HD]
            g_s[cc * npair + p] = g_q
        yield

    def recur():
        s_cur = [state[p] for p in range(npair)]
        for cc in (range(nck - 1, -1, -1) if rev else range(nck)):
            tile_rows = slice(cc * cn, (cc + 1) * cn)
            for p in range(npair):
                ls = slice(p * tw, (p + 1) * tw)
                s_b = s_cur[p].astype(BF16)
                y = _dot_nt(rhat_s[tile_rows, ls], stack(s_b)) + yhat_s[tile_rows, ls]
                if rev:
                    yhat_s[tile_rows, ls] = y + yf_ref[tile_rows, ls]
                else:
                    o_ref[tile_rows, ls] = y
                s_cur[p] = (s_cur[p] * pcp[(cc, p)]
                            + _dot(s_b, m_s[cc * npair + p]) + g_s[cc * npair + p])
            yield
        for p in range(npair):
            state[p] = s_cur[p]
        if rev:
            for g in range(ngroups):
                rows = slice(g * gr, (g + 1) * gr)
                ytot = yhat_s[rows, :]
                mu = _split_dot(ytot, bd, 2) * (1.0 / RW_HD)
                yc = ytot - mu
                var = _split_dot(yc * yc, bd, 1) * (1.0 / RW_HD)
                yn = yc * lax.rsqrt(var + GN_EPS) * lnw_ref[...] + lnb_ref[...]
                o_ref[rows, :] = ((yn + bonus_s[rows, :]) * gate_s[rows, :]).astype(BF16)
                yield

    def chain(*gens):
        for gen in gens:
            yield from gen

    def interleave(*gens):
        live = list(gens)
        while live:
            live = [gen for gen in live if next(gen, live) is not live]

    interleave(chain(*[solve(g) for g in range(ngroups)], recur()),
               chain(*[prepare(g) for g in range(ngroups)]))
    for cur, nxt in zip(staged_cur, staged_nxt):
        cur[...] = nxt[...]


def _split_dot_left(m, x):
    hi = x.astype(BF16)
    r1 = x - hi.astype(F32)
    mid = r1.astype(BF16)
    lo = (r1 - mid.astype(F32)).astype(BF16)
    return _dot(m, hi) + _dot(m, mid) + _dot(m, lo)


def _rwkv_pass(zrw, yf, wts, batch, seq, rev):
    ts = RW_STEP
    assert seq % ts == 0 and ts == RW_GROUP * RW_CHUNK and RW_CHUNK == RW_HD
    nch = seq // ts
    nck = ts // RW_CHUNK
    npair = RW_HEADS // RW_TILE_HEADS
    tw = RW_TILE_HEADS * RW_HD
    total = batch * nch

    def block(step):
        if not rev:
            return step
        within = lax.rem(step, nch)
        return step - within + (nch - 1 - within)

    zmain = pl.BlockSpec((ts, C_RW_PAD), lambda j: (block(jnp.minimum(j, total - 1)), 0))
    ymain = pl.BlockSpec((ts, D_RW), lambda j: (block(jnp.maximum(j - 1, 0)), 0))
    in_specs = [zmain] + [_const_spec(w.shape) for w in wts]
    args = [zrw] + list(wts)
    scratch = [pltpu.VMEM((npair, RW_HD, tw), F32),
               pltpu.VMEM((ts, D_RW), BF16),
               pltpu.VMEM((ts, D_RW), F32),
               pltpu.VMEM((nck * npair, tw, tw), BF16),
               pltpu.VMEM((nck * npair, RW_HD, tw), F32)]
    scratch += [pltpu.VMEM((ts, D_RW), BF16) for _ in range(7)]
    scratch += [pltpu.VMEM((nck * 8, D_RW), F32)]
    if rev:
        in_specs.append(ymain)
        args.append(yf)
        scratch += [pltpu.VMEM((ts, D_RW), F32) for _ in range(4)]
    return pl.pallas_call(
        functools.partial(_rwkv_kernel, rev=rev, nsteps=nch),
        out_shape=jax.ShapeDtypeStruct((batch * seq, D_RW), BF16 if rev else F32),
        grid=(total + 1,),
        in_specs=in_specs,
        out_specs=ymain,
        scratch_shapes=scratch,
        compiler_params=pltpu.CompilerParams(
            dimension_semantics=("arbitrary",), vmem_limit_bytes=VMEM_LIMIT),
        name="rwkv_bwd" if rev else "rwkv_fwd",
    )(*args)


def _merge_ffn_kernel(x_ref, na_ref, rw_ref, mem_ref, gate_ref, wna_ref, wrw_ref, wmem_ref, wout_ref,
                      g_ref, wg_ref, wu_ref, wd_ref, o_ref, a_scr):
    d = D_MODEL
    merged = (gate_ref[:, 0:d].astype(F32) * _dot(na_ref[...], wna_ref[...])
              + gate_ref[:, d:2 * d].astype(F32) * _dot(rw_ref[...], wrw_ref[...])
              + gate_ref[:, 2 * d:3 * d].astype(F32) * _dot(mem_ref[...], wmem_ref[...]))
    x2 = x_ref[...] + _dot(merged.astype(BF16), wout_ref[...])
    o_ref[...] = _ffn_body(x2, g_ref, wg_ref, wu_ref, wd_ref, a_scr)


def _merge_ffn(x, y_na, y_rw, y_mem, gates, wna, wrw, wmem, wout, g, wg, wu, wd):
    t = x.shape[0]
    tm = TOKEN_TILE
    tok = lambda n: pl.BlockSpec((tm, n), lambda i: (i, 0))
    return pl.pallas_call(
        _merge_ffn_kernel,
        out_shape=jax.ShapeDtypeStruct((t, D_MODEL), F32),
        grid=(t // tm,),
        in_specs=[tok(D_MODEL), tok(D_NA), tok(D_RW), tok(D_MEM), tok(C_GATE),
                  _const_spec((D_NA, D_MODEL)), _const_spec((D_RW, D_MODEL)),
                  _const_spec((D_MEM, D_MODEL)), _const_spec((D_MODEL, D_MODEL)),
                  _const_spec((1, D_MODEL)), _const_spec((D_MODEL, D_FF)),
                  _const_spec((D_MODEL, D_FF)), _const_spec((D_FF, D_MODEL))],
        out_specs=tok(D_MODEL),
        scratch_shapes=[pltpu.VMEM((tm, D_FF), BF16)],
        compiler_params=pltpu.CompilerParams(
            dimension_semantics=("parallel",), vmem_limit_bytes=VMEM_LIMIT),
        name="merge_ffn",
    )(x, y_na, y_rw, y_mem, gates, wna, wrw, wmem, wout, g, wg, wu, wd)


def _block_diag_ones(n, blk):
    idx = np.arange(n) // blk
    return jnp.asarray(idx[:, None] == idx[None, :], dtype=BF16)


def _prep_params(p):
    row = lambda a: a.reshape(1, -1).astype(F32)
    w_in = p['w_in']
    rw_lo, rw_hi = C_NA, C_NA + C_RW
    w_in_p = jnp.concatenate([
        w_in[:, :C_NA],
        jnp.pad(w_in[:, rw_lo:rw_hi], ((0, 0), (0, C_RW_PAD - C_RW))),
        w_in[:, rw_hi:],
    ], axis=1).astype(BF16)

    def lora_stack(w):
        z = jnp.zeros_like(w[0])
        return jnp.stack([jnp.concatenate([w[0], z], axis=0),
                          jnp.concatenate([z, w[1]], axis=0)]).astype(BF16)

    q = {
        'g_ffn1': row(p['g_ffn1']), 'g_ffn2': row(p['g_ffn2']), 'g_mix': row(p['g_mix']),
        'wg1': p['w_ffn1_gate'].astype(BF16), 'wu1': p['w_ffn1_up'].astype(BF16),
        'wd1': p['w_ffn1_down'].astype(BF16),
        'wg2': p['w_ffn2_gate'].astype(BF16), 'wu2': p['w_ffn2_up'].astype(BF16),
        'wd2': p['w_ffn2_down'].astype(BF16),
        'w_in_p': w_in_p,
        'gq': row(jnp.tile(p['g_qn_na'], NA_HEADS)), 'gk': row(jnp.tile(p['g_kn_na'], NA_HEADS)),
        'gmq': row(p['g_qn_mem']), 'gmk': row(p['g_kn_mem']),
        'bd64': _block_diag_ones(D_NA, NA_HD),
        'na_bias': _na_bias_table(p['rpb_na'].astype(F32)),
        'g_mem_norm': row(p['g_mem_norm']), 'w_mem_kv': p['w_mem_kv'].astype(BF16),
        'w_o_na': p['w_o_na'].astype(BF16), 'w_o_rw': p['w_o_rw'].astype(BF16),
        'w_o_mem': p['w_o_mem'].astype(BF16), 'w_out': p['w_out'].astype(BF16),
    }
    q['conv'] = jnp.pad(p['conv_rw'].astype(F32), ((0, 0), (0, C_RW_PAD - C_RW)))
    q['rw'] = (
        p['w0_rw'].astype(F32), lora_stack(p['w2_rw']),
        p['a0_rw'].astype(F32), lora_stack(p['a2_rw']),
        jnp.pad(p['g2_rw'], ((0, 2 * LANES - GATE_LORA), (0, 0))).astype(BF16),
        row(p['k_k_rw']), row(p['k_a_rw']), p['r_k_rw'].reshape(2, D_RW).astype(F32),
        row(p['ln_x_w_rw']), row(p['ln_x_b_rw']),
        q['bd64'],
    )
    return q


def _encoder_layer(x, mem, q):
    batch, seq, _ = x.shape
    x2d = x.reshape(batch * seq, D_MODEL)
    x1 = _ffn(x2d, q['g_ffn1'], q['wg1'], q['wu1'], q['wd1'])
    km, vm = _memkv(mem.reshape(batch * N_MEM, D_MODEL), q['g_mem_norm'], q['w_mem_kv'], q['gmk'])
    qn, kn, vv, zrw, y_mem, gates = _proj(x1, km, vm, q['g_mix'], q['w_in_p'], q['gq'], q['gk'], q['gmq'],
                                          q['bd64'], q['conv'], seq)
    y_na = _na(qn, kn, vv, q['na_bias'], batch, seq)
    y_fwd = _rwkv_pass(zrw, None, q['rw'], batch, seq, rev=False)
    y_rw = _rwkv_pass(zrw, y_fwd, q['rw'], batch, seq, rev=True)
    x3 = _merge_ffn(x1, y_na, y_rw, y_mem, gates, q['w_o_na'], q['w_o_rw'], q['w_o_mem'], q['w_out'],
                    q['g_ffn2'], q['wg2'], q['wu2'], q['wd2'])
    return x3.reshape(batch, seq, D_MODEL)


def kernel(x_prompt, x_sample, mem_prompt, mem_sample, g_ffn1, w_ffn1_gate, w_ffn1_up, w_ffn1_down, g_mix, w_in, g_qn_na, g_kn_na, rpb_na, w_o_na, conv_rw, w0_rw, w2_rw, a0_rw, a2_rw, g2_rw, k_k_rw, k_a_rw, r_k_rw, ln_x_w_rw, ln_x_b_rw, w_o_rw, g_mem_norm, w_mem_kv, g_qn_mem, g_kn_mem, w_o_mem, w_out, g_ffn2, w_ffn2_gate, w_ffn2_up, w_ffn2_down):
    params = {
        'g_ffn1': g_ffn1, 'w_ffn1_gate': w_ffn1_gate, 'w_ffn1_up': w_ffn1_up,
        'w_ffn1_down': w_ffn1_down, 'g_mix': g_mix, 'w_in': w_in,
        'g_qn_na': g_qn_na, 'g_kn_na': g_kn_na, 'rpb_na': rpb_na, 'w_o_na': w_o_na,
        'conv_rw': conv_rw, 'w0_rw': w0_rw, 'w2_rw': w2_rw, 'a0_rw': a0_rw, 'a2_rw': a2_rw,
        'g2_rw': g2_rw, 'k_k_rw': k_k_rw, 'k_a_rw': k_a_rw, 'r_k_rw': r_k_rw,
        'ln_x_w_rw': ln_x_w_rw, 'ln_x_b_rw': ln_x_b_rw, 'w_o_rw': w_o_rw,
        'g_mem_norm': g_mem_norm, 'w_mem_kv': w_mem_kv, 'g_qn_mem': g_qn_mem,
        'g_kn_mem': g_kn_mem, 'w_o_mem': w_o_mem, 'w_out': w_out,
        'g_ffn2': g_ffn2, 'w_ffn2_gate': w_ffn2_gate, 'w_ffn2_up': w_ffn2_up,
        'w_ffn2_down': w_ffn2_down,
    }
    depth = g_ffn1.shape[0]
    y_prompt, y_sample = x_prompt, x_sample
    for layer in range(depth):
        q = _prep_params({name: arr[layer] for name, arr in params.items()})
        y_prompt = _encoder_layer(y_prompt, mem_prompt, q)
        y_sample = _encoder_layer(y_sample, mem_sample, q)
    return (y_prompt, y_sample)
```

```python
import functools

import jax
import jax.numpy as jnp
import numpy as np
from jax import lax
from jax.experimental import pallas as pl
from jax.experimental.pallas import tpu as pltpu

F32 = jnp.float32
BF16 = jnp.bfloat16

D_MODEL = 1024
D_FF = 2816
NORM_EPS = 1e-6
GRID_W = 64
N_MEM = 256
NA_HEADS, NA_HD, NA_WIN_R, NA_WIN_C = 8, 64, 8, 16
D_NA = NA_HEADS * NA_HD
RW_HEADS, RW_HD = 8, 64
D_RW = RW_HEADS * RW_HD
DECAY_LORA, AAA_LORA, GATE_LORA = 64, 64, 160
GN_EPS = 64e-5
MEM_HEADS, MEM_HD = 4, 128
D_MEM = MEM_HEADS * MEM_HD
C_NA = 3 * D_NA
C_RW = 3 * D_RW + 2 * DECAY_LORA + 2 * AAA_LORA + GATE_LORA
C_RW_PAD = 2048
C_GATE = 3 * D_MODEL
N_IN_PAD = C_NA + C_RW_PAD + D_MEM + C_GATE

TOKEN_TILE = 512
PROJ_TILE = 512
FF_CHUNK = 256
NA_ROWS_PER_STEP = 8
NA_ROWS_UNROLL = 4
RW_CHUNK = 64
RW_STEP = 512
RW_TILE_HEADS = 2
RW_GROUP = 8
LANES = 128
VMEM_LIMIT = 56 * 1024 * 1024
NEG_BIG = -1e30


def _const_spec(shape):
    nd = len(shape)
    return pl.BlockSpec(shape, lambda *_: (0,) * nd, pipeline_mode=pl.Buffered(1))


def _dot(a, b):
    return jnp.dot(a, b, preferred_element_type=F32)


def _dot_nt(a, b):
    return lax.dot_general(a, b, (((1,), (1,)), ((), ())), preferred_element_type=F32)


def _dot_tn(a, b):
    return lax.dot_general(a, b, (((0,), (0,)), ((), ())), preferred_element_type=F32)


def _split_dot(x, m, parts):
    acc = None
    rem = x
    for i in range(parts):
        piece = rem.astype(BF16)
        term = _dot(piece, m)
        acc = term if acc is None else acc + term
        if i + 1 < parts:
            rem = rem - piece.astype(F32)
    return acc


def _rmsnorm_rows(x, g):
    ms = jnp.mean(x * x, axis=-1, keepdims=True)
    return x * lax.rsqrt(ms + NORM_EPS) * g


def _ffn_body(x, g_ref, wg_ref, wu_ref, wd_ref, a_scr):
    h = _rmsnorm_rows(x, g_ref[...]).astype(BF16)
    for f in range(0, D_FF, FF_CHUNK):
        gate = _dot(h, wg_ref[:, f:f + FF_CHUNK])
        up = _dot(h, wu_ref[:, f:f + FF_CHUNK])
        a_scr[:, f:f + FF_CHUNK] = (gate * jax.nn.sigmoid(gate) * up).astype(BF16)
    return x + 0.5 * _dot(a_scr[...], wd_ref[...])


def _ffn_kernel(x_ref, g_ref, wg_ref, wu_ref, wd_ref, o_ref, a_scr):
    o_ref[...] = _ffn_body(x_ref[...], g_ref, wg_ref, wu_ref, wd_ref, a_scr)


def _ffn(x, g, wg, wu, wd):
    t = x.shape[0]
    tm = TOKEN_TILE
    return pl.pallas_call(
        _ffn_kernel,
        out_shape=jax.ShapeDtypeStruct((t, D_MODEL), F32),
        grid=(t // tm,),
        in_specs=[
            pl.BlockSpec((tm, D_MODEL), lambda i: (i, 0)),
            _const_spec((1, D_MODEL)),
            _const_spec((D_MODEL, D_FF)),
            _const_spec((D_MODEL, D_FF)),
            _const_spec((D_FF, D_MODEL)),
        ],
        out_specs=pl.BlockSpec((tm, D_MODEL), lambda i: (i, 0)),
        scratch_shapes=[pltpu.VMEM((tm, D_FF), BF16)],
        compiler_params=pltpu.CompilerParams(
            dimension_semantics=("parallel",), vmem_limit_bytes=VMEM_LIMIT),
        name="ffn",
    )(x, g, wg, wu, wd)


def _proj_kernel(x_ref, xp_ref, xn_ref, km_ref, vm_ref, g_ref, w_ref, gq_ref, gk_ref, gmq_ref, bd_ref,
                 conv_ref, q_ref, k_ref, v_ref, zrw_ref, ymem_ref, gate_ref, *, tiles_per_seq):
    tm = PROJ_TILE
    h_all = _rmsnorm_rows(jnp.concatenate([x_ref[...], xp_ref[...], xn_ref[...]], axis=0),
                          g_ref[...]).astype(BF16)
    h = h_all[:tm]
    bd = bd_ref[...]
    i = pl.program_id(0)
    pos = i % tiles_per_seq
    has_prev = (pos > 0).astype(F32)
    has_next = (pos < tiles_per_seq - 1).astype(F32)
    row8 = lax.broadcasted_iota(jnp.int32, (8, 1), 0)

    def head_norm(z, gain):
        ms = _split_dot(z * z, bd, 1) * (1.0 / NA_HD)
        return z * lax.rsqrt(ms + NORM_EPS) * gain

    zq = _dot(h, w_ref[:, 0:D_NA])
    q_ref[...] = (head_norm(zq, gq_ref[...]) * (NA_HD ** -0.5)).astype(BF16)
    zk = _dot(h, w_ref[:, D_NA:2 * D_NA])
    k_ref[...] = head_norm(zk, gk_ref[...]).astype(BF16)
    v_ref[...] = _dot(h, w_ref[:, 2 * D_NA:3 * D_NA]).astype(BF16)

    chunks = list(range(0, C_RW_PAD, 512))
    z_next = _dot(h_all, w_ref[:, C_NA:C_NA + 512])
    for n, c in enumerate(chunks):
        z_all = z_next
        if n + 1 < len(chunks):
            z_next = _dot(h_all, w_ref[:, C_NA + c + 512:C_NA + c + 1024])
        z = z_all[:tm]
        before = z_all[tm + 7:tm + 8] * has_prev
        after = z_all[tm + 8:tm + 9] * has_next
        down = pltpu.roll(z, 1, 0)
        up = pltpu.roll(z, tm - 1, 0)
        zm1 = jnp.concatenate([jnp.where(row8 == 0, before, down[:8]), down[8:]], axis=0)
        zp1 = jnp.concatenate([up[:tm - 8], jnp.where(row8 == 7, after, up[tm - 8:])], axis=0)
        zrw_ref[:, c:c + 512] = (zm1 * conv_ref[0:1, c:c + 512] + z * conv_ref[1:2, c:c + 512]
                                 + zp1 * conv_ref[2:3, c:c + 512])

    off = C_NA + C_RW_PAD
    zm = _dot(h, w_ref[:, off:off + D_MEM])
    gmq = gmq_ref[...]
    qms = [(_rmsnorm_rows(zm[:, hd * MEM_HD:(hd + 1) * MEM_HD], gmq) * (MEM_HD ** -0.5)).astype(BF16)
           for hd in range(MEM_HEADS)]
    scores = [_dot_nt(qms[hd], km_ref[:, hd * MEM_HD:(hd + 1) * MEM_HD]) for hd in range(MEM_HEADS)]
    for hd in range(MEM_HEADS):
        sl = slice(hd * MEM_HD, (hd + 1) * MEM_HD)
        e = jnp.exp(scores[hd] - jnp.max(scores[hd], axis=-1, keepdims=True))
        inv = 1.0 / jnp.sum(e, axis=-1, keepdims=True)
        ymem_ref[:, sl] = (_dot(e.astype(BF16), vm_ref[:, sl]) * inv).astype(BF16)

    off += D_MEM
    for c in range(0, C_GATE, 512):
        zg = _dot(h, w_ref[:, off + c:off + c + 512])
        gate_ref[:, c:c + 512] = jax.nn.sigmoid(zg).astype(BF16)


def _proj(x, km, vm, g, w_in_p, gq, gk, gmq, bd64, conv_w, seq):
    t = x.shape[0]
    tm = PROJ_TILE
    assert seq % tm == 0
    sub = tm // 8
    nblk8 = t // 8
    tok = lambda n: pl.BlockSpec((tm, n), lambda i: (i, 0))
    prev = pl.BlockSpec((8, D_MODEL), lambda i: (jnp.maximum(i * sub - 1, 0), 0))
    nxt = pl.BlockSpec((8, D_MODEL), lambda i: (jnp.minimum((i + 1) * sub, nblk8 - 1), 0))
    mem_blk = pl.BlockSpec((N_MEM, D_MEM), lambda i: ((i * tm) // seq, 0))
    return pl.pallas_call(
        functools.partial(_proj_kernel, tiles_per_seq=seq // tm),
        out_shape=(
            jax.ShapeDtypeStruct((t, D_NA), BF16),
            jax.ShapeDtypeStruct((t, D_NA), BF16),
            jax.ShapeDtypeStruct((t, D_NA), BF16),
            jax.ShapeDtypeStruct((t, C_RW_PAD), F32),
            jax.ShapeDtypeStruct((t, D_MEM), BF16),
            jax.ShapeDtypeStruct((t, C_GATE), BF16),
        ),
        grid=(t // tm,),
        in_specs=[
            tok(D_MODEL), prev, nxt, mem_blk, mem_blk,
            _const_spec((1, D_MODEL)),
            _const_spec((D_MODEL, N_IN_PAD)),
            _const_spec((1, D_NA)),
            _const_spec((1, D_NA)),
            _const_spec((1, MEM_HD)),
            _const_spec((D_NA, D_NA)),
            _const_spec((3, C_RW_PAD)),
        ],
        out_specs=(tok(D_NA), tok(D_NA), tok(D_NA), tok(C_RW_PAD), tok(D_MEM), tok(C_GATE)),
        compiler_params=pltpu.CompilerParams(
            dimension_semantics=("parallel",), vmem_limit_bytes=VMEM_LIMIT),
        name="proj",
    )(x, x, x, km, vm, g, w_in_p, gq, gk, gmq, bd64, conv_w)


def _na_kernel(q_ref, k_ref, v_ref, bias_ref, o_ref, *, rows):
    j = pl.program_id(1)
    lane = lax.broadcasted_iota(jnp.int32, (GRID_W, LANES), 1)
    head0 = lane < NA_HD
    nkeys = NA_WIN_R * GRID_W

    def rows_body(it, carry):
        probs = []
        for u in range(NA_ROWS_UNROLL):
            rr = it * NA_ROWS_UNROLL + u
            r = j * NA_ROWS_PER_STEP + rr
            r0 = jnp.clip(r - NA_WIN_R // 2, 0, rows - NA_WIN_R)
            qoff = pl.multiple_of(rr * GRID_W, GRID_W)
            koff = pl.multiple_of(r0 * GRID_W, GRID_W)
            for p in range(NA_HEADS // 2):
                probs.append((r - r0, qoff, koff, p, slice(p * LANES, (p + 1) * LANES)))
        scores = []
        for d, qoff, koff, p, ls in probs:
            qp = q_ref[pl.ds(qoff, GRID_W), ls]
            zero = jnp.zeros_like(qp)
            qst = jnp.concatenate([jnp.where(head0, qp, zero), jnp.where(head0, zero, qp)], axis=0)
            scores.append(_dot_nt(qst, k_ref[pl.ds(koff, nkeys), ls]) + bias_ref[d, p])
        weights, inv_sums = [], []
        for s in scores:
            e = jnp.exp(s - jnp.max(s, axis=-1, keepdims=True))
            inv_sums.append(1.0 / jnp.sum(e, axis=-1, keepdims=True))
            weights.append(e.astype(BF16))
        outs = [_dot(w, v_ref[pl.ds(koff, nkeys), ls]) * inv
                for w, inv, (d, qoff, koff, p, ls) in zip(weights, inv_sums, probs)]
        for o, (d, qoff, koff, p, ls) in zip(outs, probs):
            o_ref[pl.ds(qoff, GRID_W), ls] = jnp.where(head0, o[:GRID_W], o[GRID_W:]).astype(BF16)
        return carry

    lax.fori_loop(0, NA_ROWS_PER_STEP // NA_ROWS_UNROLL, rows_body, 0)


def _na(q, k, v, bias, batch, seq):
    rows = seq // GRID_W
    assert rows >= NA_WIN_R and rows % NA_ROWS_PER_STEP == 0
    steps = rows // NA_ROWS_PER_STEP
    tq = NA_ROWS_PER_STEP * GRID_W
    return pl.pallas_call(
        functools.partial(_na_kernel, rows=rows),
        out_shape=jax.ShapeDtypeStruct((batch * seq, D_NA), BF16),
        grid=(batch, steps),
        in_specs=[
            pl.BlockSpec((tq, D_NA), lambda b, j: (b * steps + j, 0)),
            pl.BlockSpec((seq, D_NA), lambda b, j: (b, 0)),
            pl.BlockSpec((seq, D_NA), lambda b, j: (b, 0)),
            _const_spec(bias.shape),
        ],
        out_specs=pl.BlockSpec((tq, D_NA), lambda b, j: (b * steps + j, 0)),
        compiler_params=pltpu.CompilerParams(
            dimension_semantics=("parallel", "arbitrary"), vmem_limit_bytes=VMEM_LIMIT),
        name="natten",
    )(q, k, v, bias)


def _na_bias_table(rpb):
    wc = NA_WIN_C
    cols = np.arange(GRID_W)
    col_start = np.clip(cols - wc // 2, 0, GRID_W - wc)
    cc = np.arange(GRID_W)
    valid = (cc[None, :] >= col_start[:, None]) & (cc[None, :] < col_start[:, None] + wc)
    padded = jnp.pad(rpb, ((0, 0), (0, 0), (GRID_W - 1, GRID_W - 1)))
    by_col = jnp.stack([padded[:, :, GRID_W + NA_WIN_C - 2 - c0:2 * GRID_W + NA_WIN_C - 2 - c0]
                        for c0 in range(GRID_W)], axis=2)
    tab = jnp.stack([by_col[:, NA_WIN_R - 1 - d0:2 * NA_WIN_R - 1 - d0] for d0 in range(NA_WIN_R)],
                    axis=1)
    tab = jnp.where(valid[None, None, None], tab, NEG_BIG)
    tab = tab.transpose(1, 0, 3, 2, 4).reshape(NA_WIN_R, NA_HEADS // 2, 2 * GRID_W, NA_WIN_R * GRID_W)
    return tab.astype(F32)


def _memkv_kernel(m_ref, g_ref, w_ref, gk_ref, km_ref, vm_ref):
    h = _rmsnorm_rows(m_ref[...], g_ref[...]).astype(BF16)
    kv = _dot(h, w_ref[...])
    gk = gk_ref[...]
    for hd in range(MEM_HEADS):
        sl = slice(hd * MEM_HD, (hd + 1) * MEM_HD)
        km_ref[:, sl] = _rmsnorm_rows(kv[:, sl], gk).astype(BF16)
    vm_ref[...] = kv[:, D_MEM:].astype(BF16)


def _memkv(mem2d, g, w, gk):
    n = mem2d.shape[0]
    blk = pl.BlockSpec((N_MEM, D_MEM), lambda i: (i, 0))
    return pl.pallas_call(
        _memkv_kernel,
        out_shape=(jax.ShapeDtypeStruct((n, D_MEM), BF16), jax.ShapeDtypeStruct((n, D_MEM), BF16)),
        grid=(n // N_MEM,),
        in_specs=[
            pl.BlockSpec((N_MEM, D_MODEL), lambda i: (i, 0)),
            _const_spec((1, D_MODEL)),
            _const_spec((D_MODEL, 2 * D_MEM)),
            _const_spec((1, MEM_HD)),
        ],
        out_specs=(blk, blk),
        compiler_params=pltpu.CompilerParams(
            dimension_semantics=("parallel",), vmem_limit_bytes=VMEM_LIMIT),
        name="memkv",
    )(mem2d, g, w, gk)


def _softplus(x):
    return jnp.maximum(x, 0.0) + jnp.log(1.0 + jnp.exp(-jnp.abs(x)))


def _rwkv_kernel(z_ref, w0_ref, w2_ref, a0_ref, a2_ref, g2_ref,
                 kk_ref, ka_ref, rk_ref, lnw_ref, lnb_ref, bd_ref, *rest, rev, nsteps):
    if rev:
        yf_ref, o_ref, *scr = rest
    else:
        o_ref, *scr = rest
    state, rhat_s, yhat_s, m_s, g_s = scr[:5]
    staged = scr[5:13]
    staged_cur = scr[13:15]
    staged_nxt = scr[15:17]
    cn = RW_CHUNK
    ts = RW_STEP
    nck = ts // cn
    j = pl.program_id(0)
    d = 1 if rev else 0
    at_s, rt_s, bt_s, kt_s, bh_s, kh_s, v_s, pc_s = staged
    at_w, rt_w, bt_w, kt_w, bh_w, kh_w, v_w, pc_w = staged
    if rev:
        bonus_s, gate_s = staged_cur
        bonus_w, gate_w = staged_nxt

    @pl.when(j == 0)
    def _():
        for s in list(staged) + list(staged_cur):
            s[...] = jnp.zeros_like(s)

    @pl.when(lax.rem(jnp.maximum(j - 1, 0), nsteps) == 0)
    def _():
        state[...] = jnp.zeros_like(state)

    gr = RW_GROUP * cn
    ngroups = ts // gr
    bd = bd_ref[...]
    shift = cn.bit_length() - 1
    cb = min(gr, 4 * cn)
    ti = lax.broadcasted_iota(jnp.int32, (cb, cb), 0)
    ii = lax.broadcasted_iota(jnp.int32, (cb, cb), 1)
    incl = ((ti >> shift) == (ii >> shift)) & ((ii >= ti) if rev else (ii <= ti))
    incl_b = jnp.where(incl, 1.0, 0.0).astype(BF16)

    def prepare(g):
        r0 = g * gr
        rows = slice(r0, r0 + gr)

        def conv(lo, hi):
            return z_ref[rows, lo:hi]

        r = conv(0, D_RW)
        yield
        k = conv(D_RW, 2 * D_RW)
        yield
        v = conv(2 * D_RW, 3 * D_RW)
        v_w[rows, :] = v.astype(BF16)
        yield
        xw = conv(3 * D_RW, 3 * D_RW + LANES)
        xa = conv(3 * D_RW + LANES, 3 * D_RW + 2 * LANES).astype(BF16)

        def in_context_rate(dd):
            return jax.nn.sigmoid(a0_ref[dd:dd + 1, :] + _dot(xa, a2_ref[dd]))

        wpre = w0_ref[d:d + 1, :] + _dot(jnp.tanh(xw).astype(BF16), w2_ref[d])
        logw = -jnp.exp(-_softplus(-wpre) - 0.5)
        yield
        a = in_context_rate(d)
        kkr = k * kk_ref[...]
        kk = kkr * lax.rsqrt(jnp.maximum(_split_dot(kkr * kkr, bd, 1), 1e-24))
        yield
        kd = k * (1.0 + (a - 1.0) * ka_ref[...])
        beta = kk * a
        yield

        if rev:
            a_fwd = in_context_rate(0)
            kd_fwd = k * (1.0 + (a_fwd - 1.0) * ka_ref[...])
            bonus_w[rows, :] = _split_dot(r * (kd_fwd * rk_ref[0:1, :] + kd * rk_ref[1:2, :]), bd, 2) * v
            yield
            xg = conv(3 * D_RW + 2 * LANES, C_RW_PAD)
            gate_w[rows, :] = _dot(jax.nn.sigmoid(xg).astype(BF16), g2_ref[...])
            yield

        logp = jnp.concatenate([_split_dot_left(incl_b, logw[i:i + cb]) for i in range(0, gr, cb)], axis=0)
        ends =[cl * cn if rev else cl * cn + cn - 1 for cl in range(RW_GROUP)]
        logpc = jnp.concatenate(
            [jnp.broadcast_to(logp[e:e + 1, :], (cn, D_RW)) for e in ends], axis=0)
        for cl, e in enumerate(ends):
            cc = g * RW_GROUP + cl
            pc_w[cc * 8:(cc + 1) * 8, :] = jnp.broadcast_to(jnp.exp(logp[e:e + 1, :]), (8, D_RW))
        yield
        at_w[rows, :] = (-kk * jnp.exp(logp - logw)).astype(BF16)
        rt_w[rows, :] = (r * jnp.exp(logp)).astype(BF16)
        yield
        einv = jnp.exp(-logp)
        bt_w[rows, :] = (beta * einv).astype(BF16)
        kt_w[rows, :] = (kd * einv).astype(BF16)
        yield
        efin = jnp.exp(logpc - logp)
        bh_w[rows, :] = (beta * efin).astype(BF16)
        kh_w[rows, :] = (kd * efin).astype(BF16)
        yield

    nh = RW_TILE_HEADS
    tw = nh * RW_HD
    lane = lax.broadcasted_iota(jnp.int32, (cn, tw), 1)
    lane_head = lane >> shift

    def stack(x):
        zero = jnp.zeros_like(x)
        return jnp.concatenate([jnp.where(lane_head == hh, x, zero) for hh in range(nh)], axis=0)

    ri = lax.broadcasted_iota(jnp.int32, (tw, tw), 0)
    ci = lax.broadcasted_iota(jnp.int32, (tw, tw), 1)
    same_head = (ri >> shift) == (ci >> shift)
    t2 = lax.broadcasted_iota(jnp.int32, (cn, tw), 0)
    i2 = lane & (cn - 1)
    incl2 = (i2 >= t2) if rev else (i2 <= t2)
    strict2 = (i2 > t2) if rev else (i2 < t2)
    eye2 = jnp.where(i2 == t2, 1.0, 0.0)

    npair = RW_HEADS // nh

    def tile(ref, cc, p):
        return ref[cc * cn:(cc + 1) * cn, p * tw:(p + 1) * tw]

    vp, bhp, khp, pcp = {}, {}, {}, {}

    def solve(g):
        g0 = g * RW_GROUP
        probs = [(cc, p) for cc in range(g0, g0 + RW_GROUP) for p in range(npair)]
        ast, vst, rtp, tinv, pw, aak, arb, ark = {}, {}, {}, {}, {}, {}, {}, {}
        for q in probs:
            at_q = tile(at_s, *q)
            ast[q] = stack(at_q)
            vp[q] = tile(v_s, *q)
            vst[q] = stack(vp[q])
            rtp[q] = tile(rt_s, *q)
            bhp[q] = tile(bh_s, *q)
            khp[q] = tile(kh_s, *q)
            pcp[q] = pc_s[q[0] * 8:q[0] * 8 + 1, q[1] * tw:(q[1] + 1) * tw]
            lhs = jnp.concatenate([at_q, rtp[q]], axis=0)
            rhs = jnp.concatenate([stack(tile(bt_s, *q)), stack(tile(kt_s, *q))], axis=0)
            res = _dot_nt(lhs, rhs)
            pw[q] = jnp.where(strict2, res[:cn, :tw], 0.0)
            aak[q] = jnp.where(strict2, res[:cn, tw:], 0.0).astype(BF16)
            arb[q] = jnp.where(incl2, res[cn:, :tw], 0.0).astype(BF16)
            ark[q] = jnp.where(incl2, res[cn:, tw:], 0.0).astype(BF16)
            tinv[q] = eye2 + pw[q]
        yield
        for q in probs:
            pw_b = pw[q].astype(BF16)
            pw[q] = _dot(pw_b, stack(pw_b))
        yield
        for _ in range(4):
            for q in probs:
                pw_b = pw[q].astype(BF16)
                both = _dot(jnp.concatenate([tinv[q].astype(BF16), pw_b], axis=0), stack(pw_b))
                tinv[q] = tinv[q] + both[:cn]
                pw[q] = both[cn:]
            yield
        for q in probs:
            tinv[q] = tinv[q] + _dot(tinv[q].astype(BF16), stack(pw[q].astype(BF16)))
        yield
        wv = {q: _dot(aak[q], vst[q]).astype(BF16) for q in probs}
        yield
        ta = {q: _dot(tinv[q].astype(BF16), jnp.concatenate([ast[q], stack(wv[q])], axis=1)).astype(BF16)
              for q in probs}
        yield
        for q in probs:
            cc, p = q
            ah_q, uh_q = ta[q][:, :tw], ta[q][:, tw:]
            tile_rows = slice(cc * cn, (cc + 1) * cn)
            ls = slice(p * tw, (p + 1) * tw)
            rhs = jnp.concatenate([jnp.concatenate([stack(ah_q), stack(uh_q)], axis=1),
                                   jnp.concatenate([jnp.zeros((tw, tw), BF16), vst[q]], axis=1)], axis=0)
            ry = _dot(jnp.concatenate([arb[q], ark[q]], axis=1), rhs)
            rhat_s[tile_rows, ls] = (rtp[q].astype(F32) + ry[:, :tw]).astype(BF16)
            yhat_s[tile_rows, ls] = ry[:, tw:]
        yield
        for q in probs:
            cc, p = q
            ah_q, uh_q = ta[q][:, :tw], ta[q][:, tw:]
            bh_q = bhp[q]
            m_bd = _dot_tn(ah_q, bh_q)
            m_s[cc * npair + p] = jnp.where(same_head, m_bd, 0.0).astype(BF16)
            g_bd = _dot_tn(jnp.concatenate([uh_q, vp[q]], axis=0),
                           jnp.concatenate([bh_q, khp[q]], axis=0))
            g_bd = jnp.where(same_head, g_bd, 0.0)
            g_q = g_bd[:RW_HD]
            for hh in range(1, nh):
                g_q = g_q + g_bd[hh * RW_HD:(hh + 1) * RW_HD]
            g_s[cc * npair + p] = g_q
        yield

    def recur():
        s_cur = [state[p] for p in range(npair)]
        for cc in (range(nck - 1, -1, -1) if rev else range(nck)):
            tile_rows = slice(cc * cn, (cc + 1) * cn)
            for p in range(npair):
                ls = slice(p * tw, (p + 1) * tw)
                s_b = s_cur[p].astype(BF16)
                y = _dot_nt(rhat_s[tile_rows, ls], stack(s_b)) + yhat_s[tile_rows, ls]
                if rev:
                    yhat_s[tile_rows, ls] = y + yf_ref[tile_rows, ls]
                else:
                    o_ref[tile_rows, ls] = y
                s_cur[p] = (s_cur[p] * pcp[(cc, p)]
                            + _dot(s_b, m_s[cc * npair + p]) + g_s[cc * npair + p])
            yield
        for p in range(npair):
            state[p] = s_cur[p]
        if rev:
            for g in range(ngroups):
                rows = slice(g * gr, (g + 1) * gr)
                ytot = yhat_s[rows, :]
                mu = _split_dot(ytot, bd, 2) * (1.0 / RW_HD)
                yc = ytot - mu
                var = _split_dot(yc * yc, bd, 1) * (1.0 / RW_HD)
                yn = yc * lax.rsqrt(var + GN_EPS) * lnw_ref[...] + lnb_ref[...]
                o_ref[rows, :] = ((yn + bonus_s[rows, :]) * gate_s[rows, :]).astype(BF16)
                yield

    def chain(*gens):
        for gen in gens:
            yield from gen

    def interleave(*gens):
        live = list(gens)
        while live:
            live = [gen for gen in live if next(gen, live) is not live]

    interleave(chain(*[solve(g) for g in range(ngroups)], recur()),
               chain(*[prepare(g) for g in range(ngroups)]))
    for cur, nxt in zip(staged_cur, staged_nxt):
        cur[...] = nxt[...]


def _split_dot_left(m, x):
    hi = x.astype(BF16)
    r1 = x - hi.astype(F32)
    mid = r1.astype(BF16)
    lo = (r1 - mid.astype(F32)).astype(BF16)
    return _dot(m, hi) + _dot(m, mid) + _dot(m, lo)


def _rwkv_pass(zrw, yf, wts, batch, seq, rev):
    ts = RW_STEP
    assert seq % ts == 0 and ts == RW_GROUP * RW_CHUNK and RW_CHUNK == RW_HD
    nch = seq // ts
    nck = ts // RW_CHUNK
    npair = RW_HEADS // RW_TILE_HEADS
    tw = RW_TILE_HEADS * RW_HD
    total = batch * nch

    def block(step):
        if not rev:
            return step
        within = lax.rem(step, nch)
        return step - within + (nch - 1 - within)

    zmain = pl.BlockSpec((ts, C_RW_PAD), lambda j: (block(jnp.minimum(j, total - 1)), 0))
    ymain = pl.BlockSpec((ts, D_RW), lambda j: (block(jnp.maximum(j - 1, 0)), 0))
    in_specs = [zmain] + [_const_spec(w.shape) for w in wts]
    args = [zrw] + list(wts)
    scratch = [pltpu.VMEM((npair, RW_HD, tw), F32),
               pltpu.VMEM((ts, D_RW), BF16),
               pltpu.VMEM((ts, D_RW), F32),
               pltpu.VMEM((nck * npair, tw, tw), BF16),
               pltpu.VMEM((nck * npair, RW_HD, tw), F32)]
    scratch += [pltpu.VMEM((ts, D_RW), BF16) for _ in range(7)]
    scratch += [pltpu.VMEM((nck * 8, D_RW), F32)]
    if rev:
        in_specs.append(ymain)
        args.append(yf)
        scratch += [pltpu.VMEM((ts, D_RW), F32) for _ in range(4)]
    return pl.pallas_call(
        functools.partial(_rwkv_kernel, rev=rev, nsteps=nch),
        out_shape=jax.ShapeDtypeStruct((batch * seq, D_RW), BF16 if rev else F32),
        grid=(total + 1,),
        in_specs=in_specs,
        out_specs=ymain,
        scratch_shapes=scratch,
        compiler_params=pltpu.CompilerParams(
            dimension_semantics=("arbitrary",), vmem_limit_bytes=VMEM_LIMIT),
        name="rwkv_bwd" if rev else "rwkv_fwd",
    )(*args)


def _merge_ffn_kernel(x_ref, na_ref, rw_ref, mem_ref, gate_ref, wna_ref, wrw_ref, wmem_ref, wout_ref,
                      g_ref, wg_ref, wu_ref, wd_ref, o_ref, a_scr):
    d = D_MODEL
    merged = (gate_ref[:, 0:d].astype(F32) * _dot(na_ref[...], wna_ref[...])
              + gate_ref[:, d:2 * d].astype(F32) * _dot(rw_ref[...], wrw_ref[...])
              + gate_ref[:, 2 * d:3 * d].astype(F32) * _dot(mem_ref[...], wmem_ref[...]))
    x2 = x_ref[...] + _dot(merged.astype(BF16), wout_ref[...])
    o_ref[...] = _ffn_body(x2, g_ref, wg_ref, wu_ref, wd_ref, a_scr)


def _merge_ffn(x, y_na, y_rw, y_mem, gates, wna, wrw, wmem, wout, g, wg, wu, wd):
    t = x.shape[0]
    tm = TOKEN_TILE
    tok = lambda n: pl.BlockSpec((tm, n), lambda i: (i, 0))
    return pl.pallas_call(
        _merge_ffn_kernel,
        out_shape=jax.ShapeDtypeStruct((t, D_MODEL), F32),
        grid=(t // tm,),
        in_specs=[tok(D_MODEL), tok(D_NA), tok(D_RW), tok(D_MEM), tok(C_GATE),
                  _const_spec((D_NA, D_MODEL)), _const_spec((D_RW, D_MODEL)),
                  _const_spec((D_MEM, D_MODEL)), _const_spec((D_MODEL, D_MODEL)),
                  _const_spec((1, D_MODEL)), _const_spec((D_MODEL, D_FF)),
                  _const_spec((D_MODEL, D_FF)), _const_spec((D_FF, D_MODEL))],
        out_specs=tok(D_MODEL),
        scratch_shapes=[pltpu.VMEM((tm, D_FF), BF16)],
        compiler_params=pltpu.CompilerParams(
            dimension_semantics=("parallel",), vmem_limit_bytes=VMEM_LIMIT),
        name="merge_ffn",
    )(x, y_na, y_rw, y_mem, gates, wna, wrw, wmem, wout, g, wg, wu, wd)


def _block_diag_ones(n, blk):
    idx = np.arange(n) // blk
    return jnp.asarray(idx[:, None] == idx[None, :], dtype=BF16)


def _prep_params(p):
    row = lambda a: a.reshape(1, -1).astype(F32)
    w_in = p['w_in']
    rw_lo, rw_hi = C_NA, C_NA + C_RW
    w_in_p = jnp.concatenate([
        w_in[:, :C_NA],
        jnp.pad(w_in[:, rw_lo:rw_hi], ((0, 0), (0, C_RW_PAD - C_RW))),
        w_in[:, rw_hi:],
    ], axis=1).astype(BF16)

    def lora_stack(w):
        z = jnp.zeros_like(w[0])
        return jnp.stack([jnp.concatenate([w[0], z], axis=0),
                          jnp.concatenate([z, w[1]], axis=0)]).astype(BF16)

    q = {
        'g_ffn1': row(p['g_ffn1']), 'g_ffn2': row(p['g_ffn2']), 'g_mix': row(p['g_mix']),
        'wg1': p['w_ffn1_gate'].astype(BF16), 'wu1': p['w_ffn1_up'].astype(BF16),
        'wd1': p['w_ffn1_down'].astype(BF16),
        'wg2': p['w_ffn2_gate'].astype(BF16), 'wu2': p['w_ffn2_up'].astype(BF16),
        'wd2': p['w_ffn2_down'].astype(BF16),
        'w_in_p': w_in_p,
        'gq': row(jnp.tile(p['g_qn_na'], NA_HEADS)), 'gk': row(jnp.tile(p['g_kn_na'], NA_HEADS)),
        'gmq': row(p['g_qn_mem']), 'gmk': row(p['g_kn_mem']),
        'bd64': _block_diag_ones(D_NA, NA_HD),
        'na_bias': _na_bias_table(p['rpb_na'].astype(F32)),
        'g_mem_norm': row(p['g_mem_norm']), 'w_mem_kv': p['w_mem_kv'].astype(BF16),
        'w_o_na': p['w_o_na'].astype(BF16), 'w_o_rw': p['w_o_rw'].astype(BF16),
        'w_o_mem': p['w_o_mem'].astype(BF16), 'w_out': p['w_out'].astype(BF16),
    }
    q['conv'] = jnp.pad(p['conv_rw'].astype(F32), ((0, 0), (0, C_RW_PAD - C_RW)))
    q['rw'] = (
        p['w0_rw'].astype(F32), lora_stack(p['w2_rw']),
        p['a0_rw'].astype(F32), lora_stack(p['a2_rw']),
        jnp.pad(p['g2_rw'], ((0, 2 * LANES - GATE_LORA), (0, 0))).astype(BF16),
        row(p['k_k_rw']), row(p['k_a_rw']), p['r_k_rw'].reshape(2, D_RW).astype(F32),
        row(p['ln_x_w_rw']), row(p['ln_x_b_rw']),
        q['bd64'],
    )
    return q


def _encoder_layer(x, mem, q):
    batch, seq, _ = x.shape
    x2d = x.reshape(batch * seq, D_MODEL)
    x1 = _ffn(x2d, q['g_ffn1'], q['wg1'], q['wu1'], q['wd1'])
    km, vm = _memkv(mem.reshape(batch * N_MEM, D_MODEL), q['g_mem_norm'], q['w_mem_kv'], q['gmk'])
    qn, kn, vv, zrw, y_mem, gates = _proj(x1, km, vm, q['g_mix'], q['w_in_p'], q['gq'], q['gk'], q['gmq'],
                                          q['bd64'], q['conv'], seq)
    y_na = _na(qn, kn, vv, q['na_bias'], batch, seq)
    y_fwd = _rwkv_pass(zrw, None, q['rw'], batch, seq, rev=False)
    y_rw = _rwkv_pass(zrw, y_fwd, q['rw'], batch, seq, rev=True)
    x3 = _merge_ffn(x1, y_na, y_rw, y_mem, gates, q['w_o_na'], q['w_o_rw'], q['w_o_mem'], q['w_out'],
                    q['g_ffn2'], q['wg2'], q['wu2'], q['wd2'])
    return x3.reshape(batch, seq, D_MODEL)


def kernel(x_prompt, x_sample, mem_prompt, mem_sample, g_ffn1, w_ffn1_gate, w_ffn1_up, w_ffn1_down, g_mix, w_in, g_qn_na, g_kn_na, rpb_na, w_o_na, conv_rw, w0_rw, w2_rw, a0_rw, a2_rw, g2_rw, k_k_rw, k_a_rw, r_k_rw, ln_x_w_rw, ln_x_b_rw, w_o_rw, g_mem_norm, w_mem_kv, g_qn_mem, g_kn_mem, w_o_mem, w_out, g_ffn2, w_ffn2_gate, w_ffn2_up, w_ffn2_down):
    params = {
        'g_ffn1': g_ffn1, 'w_ffn1_gate': w_ffn1_gate, 'w_ffn1_up': w_ffn1_up,
        'w_ffn1_down': w_ffn1_down, 'g_mix': g_mix, 'w_in': w_in,
        'g_qn_na': g_qn_na, 'g_kn_na': g_kn_na, 'rpb_na': rpb_na, 'w_o_na': w_o_na,
        'conv_rw': conv_rw, 'w0_rw': w0_rw, 'w2_rw': w2_rw, 'a0_rw': a0_rw, 'a2_rw': a2_rw,
        'g2_rw': g2_rw, 'k_k_rw': k_k_rw, 'k_a_rw': k_a_rw, 'r_k_rw': r_k_rw,
        'ln_x_w_rw': ln_x_w_rw, 'ln_x_b_rw': ln_x_b_rw, 'w_o_rw': w_o_rw,
        'g_mem_norm': g_mem_norm, 'w_mem_kv': w_mem_kv, 'g_qn_mem': g_qn_mem,
        'g_kn_mem': g_kn_mem, 'w_o_mem': w_o_mem, 'w_out': w_out,
        'g_ffn2': g_ffn2, 'w_ffn2_gate': w_ffn2_gate, 'w_ffn2_up': w_ffn2_up,
        'w_ffn2_down': w_ffn2_down,
    }
    depth = g_ffn1.shape[0]
    y_prompt, y_sample = x_prompt, x_sample
    for layer in range(depth):
        q = _prep_params({name: arr[layer] for name, arr in params.items()})
        y_prompt = _encoder_layer(y_prompt, mem_prompt, q)
        y_sample = _encoder_layer(y_sample, mem_sample, q)
    return (y_prompt, y_sample)
```
